```python
import jax, jax.numpy as jnp
from jax import lax
import numpy as np

D_MODEL = 2048
BATCH = 1
SEQ = 8192
DEPTH = 2

N_META = 16
BLOCK = 128
PAD_FRONT = BLOCK - N_META
REAL_START = BLOCK
RMS_EPS = 1e-6
NEG_INF = -1e30
LOG_FLOOR = 1e-30

HA = 4
DK_A = 128
DV_A = 128
A_WIDTH = HA * DV_A
CHUNK_A = 16

HB = 16
KVH_B = 2
DH_B = 64
B_WIDTH = HB * DH_B
WINDOW = 128
ROPE_THETA = 10000.0

POOL_WINDOWS = (2, 4, 8, 16)
NG_C = 4
CG_C = 128
C_WIDTH = NG_C * CG_C

MIX_WIDTH = A_WIDTH + B_WIDTH + C_WIDTH
IN_SIZES = (HA * DK_A, HA * DK_A, A_WIDTH, A_WIDTH,
            B_WIDTH, KVH_B * DH_B, KVH_B * DH_B, B_WIDTH,
            C_WIDTH, C_WIDTH)
IN_COLS = sum(IN_SIZES)

kernel_name = "hymba_hgrn2_swa_pool_hybrid"


def rms_norm(x, g):
    x32 = x.astype(jnp.float32)
    y = x32 * lax.rsqrt(jnp.mean(x32 * x32, axis=-1, keepdims=True) + RMS_EPS)
    return (y * g.astype(jnp.float32)).astype(x.dtype)


def rope(x, pos):
    d = x.shape[-1]
    half = d // 2
    inv = jnp.power(ROPE_THETA, -jnp.arange(half, dtype=jnp.float32) * 2.0 / d)
    ang = pos.astype(jnp.float32)[:, None] * inv[None, :]
    cos = jnp.cos(ang)[None, :, None, :]
    sin = jnp.sin(ang)[None, :, None, :]
    x32 = x.astype(jnp.float32)
    x1, x2 = x32[..., :half], x32[..., half:]
    out = jnp.concatenate([x1 * cos - x2 * sin, x2 * cos + x1 * sin], axis=-1)
    return out.astype(x.dtype)


def hgrn2_mixer(q, z_f, i_in, lb, valid):
    b, p = q.shape[0], q.shape[1]
    n = p // CHUNK_A
    lb32 = lb.astype(jnp.float32)
    z = z_f.astype(jnp.float32)
    m = valid[None, :, None]
    sig = jax.nn.sigmoid(z)
    f = lb32 + (1.0 - lb32) * sig
    log_f = jnp.log(jnp.maximum(f, LOG_FLOOR)) * m
    k = (1.0 - lb32) * (1.0 - sig) * m
    qf = jax.nn.silu(q.astype(jnp.float32))
    v = i_in.astype(jnp.float32)

    def heads(t, d):
        return t.reshape(b, n, CHUNK_A, HA, d).transpose(0, 3, 1, 2, 4)

    qf, k, log_f, v = heads(qf, DK_A), heads(k, DK_A), heads(log_f, DK_A), heads(v, DV_A)
    G = jnp.cumsum(log_f, axis=3)
    G_last = G[:, :, :, -1:, :]

    causal = jnp.tril(jnp.ones((CHUNK_A, CHUNK_A), dtype=bool))
    diff = G[:, :, :, :, None, :] - G[:, :, :, None, :, :]
    decay = jnp.exp(jnp.where(causal[:, :, None], diff, NEG_INF))
    attn = jnp.einsum('bhnik,bhnijk->bhnij', qf, decay * k[:, :, :, None, :, :])
    o_intra = jnp.einsum('bhnij,bhnjv->bhniv', attn, v)

    q_dec = qf * jnp.exp(G)
    k_dec = k * jnp.exp(G_last - G)
    a = jnp.exp(G_last[:, :, :, 0, :])

    def step(S, xs):
        qn, kn, vn, an = xs
        o = jnp.einsum('bhck,bhkv->bhcv', qn, S)
        S = an[..., None] * S + jnp.einsum('bhck,bhcv->bhkv', kn, vn)
        return S, o

    xs = (jnp.moveaxis(q_dec, 2, 0), jnp.moveaxis(k_dec, 2, 0),
          jnp.moveaxis(v, 2, 0), jnp.moveaxis(a, 2, 0))
    S0 = jnp.zeros((b, HA, DK_A, DV_A), jnp.float32)
    _, o_inter = lax.scan(step, S0, xs)
    return o_intra + jnp.moveaxis(o_inter, 0, 2)


def swa_sink_attention(q, k, v, sinks):
    b, p = q.shape[0], q.shape[1]
    nb = p // BLOCK
    grp = HB // KVH_B
    qb = q.reshape(b, nb, BLOCK, KVH_B, grp, DH_B)

    def band(t):
        tb = t.reshape(b, nb, BLOCK, KVH_B, DH_B)
        prev = jnp.pad(tb, ((0, 0), (1, 0), (0, 0), (0, 0), (0, 0)))[:, :nb]
        meta = jnp.broadcast_to(t[:, None, PAD_FRONT:REAL_START], (b, nb, N_META, KVH_B, DH_B))
        return jnp.concatenate([meta, prev, tb], axis=2)

    k_all, v_all = band(k), band(v)
    s = jnp.einsum('bnqkgd,bnskd->bnkgqs', qb, k_all).astype(jnp.float32) * (DH_B ** -0.5)

    q_idx = jnp.arange(nb)[:, None] * BLOCK + jnp.arange(BLOCK)[None, :]
    band_idx = jnp.arange(nb)[:, None] * BLOCK - BLOCK + jnp.arange(2 * BLOCK)[None, :]
    meta_idx = jnp.broadcast_to(PAD_FRONT + jnp.arange(N_META)[None, :], (nb, N_META))
    key_idx = jnp.concatenate([meta_idx, band_idx], axis=1)
    is_meta = jnp.concatenate([jnp.ones((N_META,), bool), jnp.zeros((2 * BLOCK,), bool)])
    qi = q_idx[:, :, None]
    kj = key_idx[:, None, :]
    in_band = (kj > qi - WINDOW) & (kj >= REAL_START)
    allowed = (kj <= qi) & (is_meta[None, None, :] | in_band)
    s = jnp.where(allowed[None, :, None, None], s, NEG_INF)

    sink = sinks.astype(jnp.float32).reshape(KVH_B, grp)[None, None, :, :, None, None]
    mx = jnp.maximum(jnp.max(s, axis=-1, keepdims=True), sink)
    pr = jnp.exp(s - mx)
    denom = jnp.sum(pr, axis=-1, keepdims=True) + jnp.exp(sink - mx)
    w = (pr / denom).astype(v.dtype)
    o = jnp.einsum('bnkgqs,bnskd->bnqkgd', w, v_all)
    return o.reshape(b, p, HB * DH_B)


def multiscale_pool(u, valid, pool_w, pool_scale):
    b, p = u.shape[0], u.shape[1]
    ug = (u.astype(jnp.float32) * valid[None, :, None]).reshape(b, p, NG_C, CG_C)
    cs = jnp.cumsum(ug, axis=1)
    cnt = jnp.cumsum(valid)
    outs = []
    for g, w in enumerate(POOL_WINDOWS):
        cs_g = cs[:, :, g]
        cs_prev = jnp.pad(cs_g, ((0, 0), (w, 0), (0, 0)))[:, :p]
        cnt_prev = jnp.pad(cnt, (w, 0))[:p]
        mean = (cs_g - cs_prev) / jnp.maximum(cnt - cnt_prev, 1.0)[None, :, None]
        outs.append((mean - ug[:, :, g]) * valid[None, :, None])
    pooled = jnp.stack(outs, axis=2)
    y = jnp.einsum('bpgc,gcd->bpgd', pooled, pool_w.astype(jnp.float32))
    return (y.reshape(b, p, C_WIDTH) * pool_scale.astype(jnp.float32)).astype(u.dtype)


def hybrid_layer(h, norm_g, w_in, lb, q_norm_g, k_norm_g, sinks, hgrn_norm_g,
                 pool_w, pool_scale, w_out, pos, valid):
    b, p = h.shape[0], h.shape[1]
    xn = rms_norm(h, norm_g)
    proj = xn @ w_in
    splits = np.cumsum(IN_SIZES)[:-1].tolist()
    qa, fa, ia, ga, qb, kb, vb, gb, uc, gc = jnp.split(proj, splits, axis=-1)

    oa = hgrn2_mixer(qa, fa, ia, lb, valid)
    oa = oa.transpose(0, 2, 3, 1, 4).reshape(b, p, HA, DV_A)
    ya = rms_norm(oa, hgrn_norm_g).reshape(b, p, A_WIDTH).astype(h.dtype)

    qh = rope(rms_norm(qb.reshape(b, p, HB, DH_B), q_norm_g), pos)
    kh = rope(rms_norm(kb.reshape(b, p, KVH_B, DH_B), k_norm_g), pos)
    vh = vb.reshape(b, p, KVH_B, DH_B)
    yb = swa_sink_attention(qh, kh, vh, sinks).astype(h.dtype)

    yc = multiscale_pool(uc, valid, pool_w, pool_scale)

    mixed = jnp.concatenate([ya * jax.nn.silu(ga), yb * jax.nn.silu(gb),
                             yc * jax.nn.silu(gc)], axis=-1)
    return h + mixed @ w_out


def setup_inputs(seed: int = 0) -> dict:
    key = jax.random.key(seed)
    ks = jax.random.split(key, 12)
    f32 = jnp.float32
    return {
        "x": jax.random.normal(ks[0], (BATCH, SEQ, D_MODEL), f32),
        "meta_tokens": jax.random.normal(ks[1], (N_META, D_MODEL), f32),
        "lb_logits": 0.5 * jax.random.normal(ks[2], (DEPTH, HA * DK_A), f32),
        "norm_g": 1.0 + 0.02 * jax.random.normal(ks[3], (DEPTH, D_MODEL), f32),
        "w_in": jax.random.normal(ks[4], (DEPTH, D_MODEL, IN_COLS), f32) * D_MODEL ** -0.5,
        "q_norm_g": 1.0 + 0.02 * jax.random.normal(ks[5], (DEPTH, DH_B), f32),
        "k_norm_g": 1.0 + 0.02 * jax.random.normal(ks[6], (DEPTH, DH_B), f32),
        "attn_sinks": 0.5 * jax.random.normal(ks[7], (DEPTH, HB), f32),
        "hgrn_norm_g": 1.0 + 0.02 * jax.random.normal(ks[8], (DEPTH, DV_A), f32),
        "pool_w": jax.random.normal(ks[9], (DEPTH, NG_C, CG_C, CG_C), f32) * CG_C ** -0.5,
        "pool_scale": 1.0 + 0.02 * jax.random.normal(ks[10], (DEPTH, C_WIDTH), f32),
        "w_out": jax.random.normal(ks[11], (DEPTH, MIX_WIDTH, D_MODEL), f32) * MIX_WIDTH ** -0.5,
    }


def reference(x, meta_tokens, lb_logits, norm_g, w_in, q_norm_g, k_norm_g, attn_sinks,
              hgrn_norm_g, pool_w, pool_scale, w_out):
    b = x.shape[0]
    h = jnp.concatenate([
        jnp.zeros((b, PAD_FRONT, D_MODEL), x.dtype),
        jnp.broadcast_to(meta_tokens.astype(x.dtype)[None], (b, N_META, D_MODEL)),
        x], axis=1)
    p = h.shape[1]
    idx = jnp.arange(p)
    valid = (idx >= PAD_FRONT).astype(jnp.float32)
    pos = idx - PAD_FRONT

    sm = jax.nn.softmax(lb_logits.astype(jnp.float32), axis=0)
    lb_all = jnp.cumsum(sm, axis=0) - sm[0:1]

    for l in range(DEPTH):
        h = hybrid_layer(h, norm_g[l], w_in[l], lb_all[l], q_norm_g[l], k_norm_g[l],
                         attn_sinks[l], hgrn_norm_g[l], pool_w[l], pool_scale[l],
                         w_out[l], pos, valid)
    return h[:, REAL_START:]
```

```python
import functools

import numpy as np
import jax
import jax.numpy as jnp
from jax import lax
from jax.experimental import pallas as pl
from jax.experimental.pallas import tpu as pltpu

F32 = jnp.float32
BF16 = jnp.bfloat16

N_META = 16
TILE = 128
PAD_FRONT = TILE - N_META
RMS_EPS = 1e-6
NEG_INF = -1e30
LOG_FLOOR = 1e-30
ROPE_THETA = 10000.0

HA, DK_A = 4, 128
A_WIDTH = HA * DK_A
HB, KVH_B, DH_B = 16, 2, 64
B_WIDTH = HB * DH_B
KV_WIDTH = KVH_B * DH_B
POOL_WINDOWS = (2, 4, 8, 16)
CG_C = 128
C_WIDTH = len(POOL_WINDOWS) * CG_C
MIX_WIDTH = A_WIDTH + B_WIDTH + C_WIDTH

OFF_QA = 0
OFF_FA = OFF_QA + A_WIDTH
OFF_IA = OFF_FA + A_WIDTH
OFF_GA = OFF_IA + A_WIDTH
OFF_QB = OFF_GA + A_WIDTH
OFF_KB = OFF_QB + B_WIDTH
OFF_VB = OFF_KB + KV_WIDTH
OFF_GB = OFF_VB + KV_WIDTH
OFF_UC = OFF_GB + B_WIDTH
OFF_GC = OFF_UC + C_WIDTH
IN_COLS = OFF_GC + C_WIDTH
PROJ_CHUNK = 768

LANES = 128
QB = 64
KWIN = 256
KBUF_ROWS = 320
LEVELS = (128, 64, 32, 16, 8, 4, 2)
VMEM_LIMIT = 50 * 1024 * 1024


def _decay_constants():
    r = np.arange(TILE)
    mats, masks = [], []
    for blk in LEVELS:
        half = blk // 2
        pos = r % blk
        piv = (r // blk) * blk + half - 1
        m = np.zeros((TILE, TILE), np.float32)
        for i in range(TILE):
            if pos[i] >= half:
                m[i, piv[i] + 1:i + 1] = 1.0
            else:
                m[i, i + 1:piv[i] + 1] = 1.0
        mats.append(m)
        same = (r[:, None] // blk) == (r[None, :] // blk)
        masks.append((same & (pos[:, None] >= half) & (pos[None, :] < half)).astype(np.float32))
    mats.append(np.tril(np.ones((TILE, TILE), np.float32)))
    mats.append(np.triu(np.ones((TILE, TILE), np.float32), 1))
    masks.append(np.eye(TILE, dtype=np.float32))
    return np.concatenate(mats, axis=0), np.stack(masks, axis=0)


def _dot(a, b):
    return jnp.dot(a, b, preferred_element_type=F32)


def _dot_nt(a, b):
    return lax.dot_general(a, b, (((1,), (1,)), ((), ())), preferred_element_type=F32)


def _dot_tn(a, b):
    return lax.dot_general(a, b, (((0,), (0,)), ((), ())), preferred_element_type=F32)


def _silu(x):
    return x * jax.nn.sigmoid(x)


def _layer_kernel(layer, sinks_ref, h_ref, cos_ref, sin_ref, ng_ref, win_ref, lbl_ref,
                  qg_ref, kg_ref, hg_ref, pw_ref, ps_ref, wout_ref, mstack_ref, lmask_ref,
                  out_ref, proj_ref, mixed_ref, expo_ref, st_ref, kbuf_ref, ksw_ref, vbuf_ref,
                  vsw_ref, uext_ref):
    t = pl.program_id(0)

    @pl.when(t == 0)
    def _zero_state():
        st_ref[...] = jnp.zeros_like(st_ref)
        kbuf_ref[...] = jnp.zeros_like(kbuf_ref)
        ksw_ref[...] = jnp.zeros_like(ksw_ref)
        vbuf_ref[...] = jnp.zeros_like(vbuf_ref)
        vsw_ref[...] = jnp.zeros_like(vsw_ref)
        uext_ref[...] = jnp.zeros_like(uext_ref)

    row = t * TILE + lax.broadcasted_iota(jnp.int32, (TILE, 1), 0)
    valid = (row >= PAD_FRONT).astype(F32)

    h = h_ref[...]
    ms = jnp.mean(h * h, axis=-1, keepdims=True)
    xn = (h * lax.rsqrt(ms + RMS_EPS) * ng_ref[...]).astype(BF16)
    for c0 in range(0, IN_COLS, PROJ_CHUNK):
        proj_ref[:, c0:c0 + PROJ_CHUNK] = _dot(xn, win_ref[:, c0:c0 + PROJ_CHUNK])

    rows = [lbl_ref[j:j + 1, :] for j in range(lbl_ref.shape[0])]
    mx = functools.reduce(jnp.maximum, rows)
    es = [jnp.exp(x - mx) for x in rows]
    lb = sum(es[1:layer + 1], jnp.zeros_like(mx)) / sum(es)

    z = proj_ref[:, OFF_FA:OFF_FA + A_WIDTH]
    sig = jax.nn.sigmoid(z)
    f = lb + (1.0 - lb) * sig
    logf = jnp.log(jnp.maximum(f, LOG_FLOOR)) * valid
    kk = (1.0 - lb) * (1.0 - sig) * valid
    proj_ref[:, OFF_QA:OFF_QA + A_WIDTH] = _silu(proj_ref[:, OFF_QA:OFF_QA + A_WIDTH])
    proj_ref[:, OFF_FA:OFF_FA + A_WIDTH] = kk
    lf_hi = logf.astype(BF16)
    lf_lo = (logf - lf_hi.astype(F32)).astype(BF16)
    mstack = mstack_ref[...]
    expo_ref[...] = _dot(mstack, lf_hi) + _dot(mstack, lf_lo)
    n_lv = len(LEVELS)
    rloc = lax.broadcasted_iota(jnp.int32, (TILE, 1), 0)

    for hd in range(HA):
        cs = slice(hd * DK_A, (hd + 1) * DK_A)
        qf_h = proj_ref[:, OFF_QA + hd * DK_A:OFF_QA + (hd + 1) * DK_A]
        kk_h = proj_ref[:, OFF_FA + hd * DK_A:OFF_FA + (hd + 1) * DK_A]
        v_h = proj_ref[:, OFF_IA + hd * DK_A:OFF_IA + (hd + 1) * DK_A].astype(BF16)
        attn = _dot_nt(qf_h.astype(BF16), kk_h.astype(BF16)) * lmask_ref[n_lv]
        for li, blk in enumerate(LEVELS):
            upper = (rloc & (blk - 1)) >= (blk // 2)
            xb = (jnp.where(upper, qf_h, kk_h) * jnp.exp(expo_ref[li * TILE:(li + 1) * TILE, cs])).astype(BF16)
            attn = attn + _dot_nt(xb, xb) * lmask_ref[li]
        o = _dot(attn.astype(BF16), v_h)
        st = st_ref[hd]
        g_inc = expo_ref[n_lv * TILE:(n_lv + 1) * TILE, cs]
        q_dec = (qf_h * jnp.exp(g_inc)).astype(BF16)
        o = o + _dot_nt(q_dec, st.astype(BF16))
        k_dec = (kk_h * jnp.exp(expo_ref[(n_lv + 1) * TILE:(n_lv + 2) * TILE, cs])).astype(BF16)
        a_tot = jnp.exp(g_inc[TILE - 1:TILE])
        st_ref[hd] = a_tot * st + _dot_tn(v_h, k_dec)
        ya = o * lax.rsqrt(jnp.mean(o * o, axis=-1, keepdims=True) + RMS_EPS) * hg_ref[...]
        ga = proj_ref[:, OFF_GA + hd * DK_A:OFF_GA + (hd + 1) * DK_A]
        mixed_ref[:, hd * DK_A:(hd + 1) * DK_A] = (ya * _silu(ga)).astype(BF16)

    lane = lax.broadcasted_iota(jnp.int32, (1, LANES), 1)
    lo_half = lane < DH_B
    first = (lane & (DH_B - 1)) < DH_B // 2
    cos = cos_ref[...]
    sin = sin_ref[...]

    def norm_rope(x, g):
        sq = x * x
        s0 = jnp.sum(jnp.where(lo_half, sq, 0.0), axis=-1, keepdims=True)
        s1 = jnp.sum(jnp.where(lo_half, 0.0, sq), axis=-1, keepdims=True)
        msq = jnp.where(lo_half, s0, s1) * (1.0 / DH_B)
        y = x * lax.rsqrt(msq + RMS_EPS) * g
        rot = jnp.where(first, pltpu.roll(y, LANES - DH_B // 2, axis=1),
                        pltpu.roll(y, DH_B // 2, axis=1))
        return y * cos + rot * sin

    kc = norm_rope(proj_ref[:, OFF_KB:OFF_KB + KV_WIDTH], kg_ref[...])
    vc = proj_ref[:, OFF_VB:OFF_VB + KV_WIDTH]
    kc_sw = pltpu.roll(kc, DH_B, axis=1)
    vc_sw = pltpu.roll(vc, DH_B, axis=1)
    cur0 = N_META + TILE
    kbuf_ref[cur0:cur0 + TILE, :] = kc.astype(BF16)
    ksw_ref[cur0:cur0 + TILE, :] = kc_sw.astype(BF16)
    vbuf_ref[cur0:cur0 + TILE, :] = vc.astype(BF16)
    vsw_ref[cur0:cur0 + TILE, :] = vc_sw.astype(BF16)

    @pl.when(t == 0)
    def _store_meta():
        for m0 in (0, cur0 + TILE):
            kbuf_ref[m0:m0 + N_META, :] = kc[PAD_FRONT:].astype(BF16)
            ksw_ref[m0:m0 + N_META, :] = kc_sw[PAD_FRONT:].astype(BF16)
            vbuf_ref[m0:m0 + N_META, :] = vc[PAD_FRONT:].astype(BF16)
            vsw_ref[m0:m0 + N_META, :] = vc_sw[PAD_FRONT:].astype(BF16)

    scale = DH_B ** -0.5
    qcols = []
    for j in range(HB // 2):
        qj = norm_rope(proj_ref[:, OFF_QB + j * LANES:OFF_QB + (j + 1) * LANES], qg_ref[...]) * scale
        qcols.append((jnp.where(lo_half, qj, 0.0).astype(BF16), jnp.where(lo_half, 0.0, qj).astype(BF16)))

    grp = HB // KVH_B
    heads_same = [hh for hh in range(HB) if (hh % 2) == (hh // grp)]
    heads_swap = [hh for hh in range(HB) if (hh % 2) != (hh // grp)]
    col = lax.broadcasted_iota(jnp.int32, (1, KWIN), 1)
    big = jnp.int32(1 << 30)
    for sb in range(TILE // QB):
        r0 = sb * QB
        w0 = sb * QB
        if sb == 0:
            is_meta = col < N_META
            kj = jnp.where(is_meta, PAD_FRONT + col,
                           jnp.where(col < cur0, (t - 1) * TILE + col - N_META, t * TILE + col - cur0))
        else:
            n_prev = cur0 - w0
            m_lo = n_prev + TILE
            is_meta = (col >= m_lo) & (col < m_lo + N_META)
            kj = jnp.where(col < n_prev, (t - 1) * TILE + (TILE - n_prev) + col,
                           jnp.where(col < m_lo, t * TILE + col - n_prev,
                                     jnp.where(is_meta, PAD_FRONT + col - m_lo, big)))
        qi = t * TILE + r0 + lax.broadcasted_iota(jnp.int32, (QB, 1), 0)
        allowed = (kj <= qi) & (is_meta | ((kj > qi - TILE) & (kj >= TILE)))

        outs = {}
        for heads, k_ref, v_ref in ((heads_same, kbuf_ref, vbuf_ref), (heads_swap, ksw_ref, vsw_ref)):
            lhs = jnp.concatenate([qcols[hh // 2][hh % 2][r0:r0 + QB] for hh in heads], axis=0)
            s_all = _dot_nt(lhs, k_ref[w0:w0 + KWIN, :])
            ps = []
            for n, hh in enumerate(heads):
                s = jnp.where(allowed, s_all[n * QB:(n + 1) * QB], NEG_INF)
                sink = sinks_ref[hh]
                mxs = jnp.maximum(jnp.max(s, axis=-1, keepdims=True), sink)
                pr = jnp.exp(s - mxs)
                den = jnp.sum(pr, axis=-1, keepdims=True) + jnp.exp(sink - mxs)
                ps.append((pr * (1.0 / den)).astype(BF16))
            o_all = _dot(jnp.concatenate(ps, axis=0), v_ref[w0:w0 + KWIN, :])
            for n, hh in enumerate(heads):
                outs[hh] = o_all[n * QB:(n + 1) * QB]
        for j in range(HB // 2):
            yb = jnp.where(lo_half, outs[2 * j], outs[2 * j + 1])
            gb = proj_ref[r0:r0 + QB, OFF_GB + j * LANES:OFF_GB + (j + 1) * LANES]
            mixed_ref[r0:r0 + QB, A_WIDTH + j * LANES:A_WIDTH + (j + 1) * LANES] = (yb * _silu(gb)).astype(BF16)

    kbuf_ref[N_META:cur0, :] = kbuf_ref[cur0:cur0 + TILE, :]
    ksw_ref[N_META:cur0, :] = ksw_ref[cur0:cur0 + TILE, :]
    vbuf_ref[N_META:cur0, :] = vbuf_ref[cur0:cur0 + TILE, :]
    vsw_ref[N_META:cur0, :] = vsw_ref[cur0:cur0 + TILE, :]

    wmax = max(POOL_WINDOWS)
    ug = proj_ref[:, OFF_UC:OFF_UC + C_WIDTH] * valid
    uext_ref[wmax:wmax + TILE, :] = ug
    n_valid = jnp.maximum(row - (PAD_FRONT - 1), 0)
    for g, w in enumerate(POOL_WINDOWS):
        cs = slice(g * CG_C, (g + 1) * CG_C)
        acc = uext_ref[wmax:wmax + TILE, cs]
        for s in range(1, w):
            acc = acc + uext_ref[wmax - s:wmax - s + TILE, cs]
        cnt = (n_valid - jnp.maximum(row - w - (PAD_FRONT - 1), 0)).astype(F32)
        pooled = (acc / jnp.maximum(cnt, 1.0) - ug[:, cs]) * valid
        y = _dot(pooled.astype(BF16), pw_ref[g]) * ps_ref[:, cs]
        gc = proj_ref[:, OFF_GC + g * CG_C:OFF_GC + (g + 1) * CG_C]
        mixed_ref[:, A_WIDTH + B_WIDTH + g * CG_C:A_WIDTH + B_WIDTH + (g + 1) * CG_C] = (y * _silu(gc)).astype(BF16)
    uext_ref[0:wmax, :] = ug[TILE - wmax:]

    out_ref[...] = h + _dot(mixed_ref[...], wout_ref[...])


def _layer_call(layer, h, cos, sin, sinks, norm_g, w_in, lb_logits, qg, kg, hg, pool_w, pool_scale,
                w_out, mstack, lmask):
    p, d = h.shape
    n_tiles = p // TILE

    def rows(width):
        return pl.BlockSpec((TILE, width), lambda t: (t, 0))

    def whole(shape, single=False):
        idx = lambda t: (0,) * len(shape)
        if single:
            return pl.BlockSpec(shape, idx, pipeline_mode=pl.Buffered(1))
        return pl.BlockSpec(shape, idx)

    in_specs = [
        pl.BlockSpec(memory_space=pltpu.SMEM),
        rows(d), rows(LANES), rows(LANES),
        whole(norm_g.shape),
        whole(w_in.shape, single=True),
        whole(lb_logits.shape),
        whole(qg.shape), whole(kg.shape), whole(hg.shape),
        whole(pool_w.shape), whole(pool_scale.shape),
        whole(w_out.shape, single=True),
        whole(mstack.shape, single=True), whole(lmask.shape, single=True),
    ]
    scratch = [
        pltpu.VMEM((TILE, IN_COLS), F32),
        pltpu.VMEM((TILE, MIX_WIDTH), BF16),
        pltpu.VMEM(((len(LEVELS) + 2) * TILE, A_WIDTH), F32),
        pltpu.VMEM((HA, DK_A, DK_A), F32),
        pltpu.VMEM((KBUF_ROWS, LANES), BF16),
        pltpu.VMEM((KBUF_ROWS, LANES), BF16),
        pltpu.VMEM((KBUF_ROWS, LANES), BF16),
        pltpu.VMEM((KBUF_ROWS, LANES), BF16),
        pltpu.VMEM((max(POOL_WINDOWS) + TILE, C_WIDTH), F32),
    ]
    return pl.pallas_call(
        functools.partial(_layer_kernel, layer),
        grid=(n_tiles,),
        in_specs=in_specs,
        out_specs=rows(d),
        out_shape=jax.ShapeDtypeStruct((p, d), F32),
        scratch_shapes=scratch,
        compiler_params=pltpu.CompilerParams(
            dimension_semantics=("arbitrary",), vmem_limit_bytes=VMEM_LIMIT),
        name=f"hybrid_layer{layer}",
    )(sinks, h, cos, sin, norm_g, w_in, lb_logits, qg, kg, hg, pool_w, pool_scale, w_out, mstack, lmask)


def kernel(x, meta_tokens, lb_logits, norm_g, w_in, q_norm_g, k_norm_g, attn_sinks, hgrn_norm_g,
           pool_w, pool_scale, w_out):
    b, seq, d = x.shape
    depth = w_in.shape[0]
    assert b == 1 and seq % TILE == 0
    assert w_in.shape[2] == IN_COLS and w_out.shape[1] == MIX_WIDTH
    assert meta_tokens.shape[0] == N_META

    h = jnp.concatenate([jnp.zeros((PAD_FRONT, d), x.dtype), meta_tokens.astype(x.dtype), x[0]], axis=0)
    p = h.shape[0]

    pos = (jnp.arange(p) - PAD_FRONT).astype(F32)
    half = DH_B // 2
    inv = jnp.power(ROPE_THETA, -jnp.arange(half, dtype=F32) * 2.0 / DH_B)
    ang = pos[:, None] * inv[None, :]
    cos = jnp.tile(jnp.cos(ang), (1, LANES // half))
    sin = jnp.tile(jnp.concatenate([-jnp.sin(ang), jnp.sin(ang)], axis=1), (1, LANES // DH_B))

    mstack_np, lmask_np = _decay_constants()
    mstack = jnp.asarray(mstack_np, BF16)
    lmask = jnp.asarray(lmask_np, F32)

    for l in range(depth):
        h = _layer_call(
            l, h, cos, sin, attn_sinks[l],
            norm_g[l][None, :], w_in[l].astype(BF16), lb_logits,
            jnp.tile(q_norm_g[l], LANES // DH_B)[None, :], jnp.tile(k_norm_g[l], LANES // DH_B)[None, :],
            hgrn_norm_g[l][None, :], pool_w[l].astype(BF16), pool_scale[l][None, :],
            w_out[l].astype(BF16), mstack, lmask)
    return h[TILE:][None]
```

```python
import functools

import numpy as np
import jax
import jax.numpy as jnp
from jax import lax
from jax.experimental import pallas as pl
from jax.experimental.pallas import tpu as pltpu

F32 = jnp.float32
BF16 = jnp.bfloat16

N_META = 16
TILE = 128
PAD_FRONT = TILE - N_META
LAG_MIX, LAG_OUT = 1, 2
RMS_EPS = 1e-6
NEG_INF = -1e30
LOG_FLOOR = 1e-30
ROPE_THETA = 10000.0

HA, DK_A = 4, 128
A_WIDTH = HA * DK_A
HB, KVH_B, DH_B = 16, 2, 64
B_WIDTH = HB * DH_B
KV_WIDTH = KVH_B * DH_B
POOL_WINDOWS = (2, 4, 8, 16)
CG_C = 128
C_WIDTH = len(POOL_WINDOWS) * CG_C
MIX_WIDTH = A_WIDTH + B_WIDTH + C_WIDTH

OFF_QA = 0
OFF_FA = OFF_QA + A_WIDTH
OFF_IA = OFF_FA + A_WIDTH
OFF_GA = OFF_IA + A_WIDTH
OFF_QB = OFF_GA + A_WIDTH
OFF_KB = OFF_QB + B_WIDTH
OFF_VB = OFF_KB + KV_WIDTH
OFF_GB = OFF_VB + KV_WIDTH
OFF_UC = OFF_GB + B_WIDTH
OFF_GC = OFF_UC + C_WIDTH
IN_COLS = OFF_GC + C_WIDTH
PROJ_CHUNK = 768

LANES = 128
QB = 64
KWIN = 256
KBUF_ROWS = 320
LEVELS = (128, 64, 32, 16, 8, 4, 2)
VMEM_LIMIT = 56 * 1024 * 1024


def _decay_constants():
    r = np.arange(TILE)
    mats, masks = [], []
    for blk in LEVELS:
        half = blk // 2
        pos = r % blk
        piv = (r // blk) * blk + half - 1
        m = np.zeros((TILE, TILE), np.float32)
        for i in range(TILE):
            if pos[i] >= half:
                m[i, piv[i] + 1:i + 1] = 1.0
            else:
                m[i, i + 1:piv[i] + 1] = 1.0
        mats.append(m)
        same = (r[:, None] // blk) == (r[None, :] // blk)
        masks.append((same & (pos[:, None] >= half) & (pos[None, :] < half)).astype(np.float32))
    mats.append(np.tril(np.ones((TILE, TILE), np.float32)))
    mats.append(np.triu(np.ones((TILE, TILE), np.float32), 1))
    masks.append(np.eye(TILE, dtype=np.float32))
    return np.concatenate(mats, axis=0), np.stack(masks, axis=0)


def _dot(a, b):
    return jnp.dot(a, b, preferred_element_type=F32)


def _dot_nt(a, b):
    return lax.dot_general(a, b, (((1,), (1,)), ((), ())), preferred_element_type=F32)


def _dot_tn(a, b):
    return lax.dot_general(a, b, (((0,), (0,)), ((), ())), preferred_element_type=F32)


def _silu(x):
    return x * jax.nn.sigmoid(x)


def _mix_tile(layer, blk, sinks_ref, cos_ref, sin_ref, lbl_ref, qg_ref, kg_ref, hg_ref,
              pw_ref, ps_ref, mstack_ref, lmask_ref, proj_ref, mixed_ref, expo_ref, st_ref,
              kbuf_ref, ksw_ref, vbuf_ref, vsw_ref, uext_ref):
    rloc = lax.broadcasted_iota(jnp.int32, (TILE, 1), 0)
    row = blk * TILE + rloc
    valid = (row >= PAD_FRONT).astype(F32)

    rows = [lbl_ref[j:j + 1, :] for j in range(lbl_ref.shape[0])]
    mx = functools.reduce(jnp.maximum, rows)
    es = [jnp.exp(x - mx) for x in rows]
    lb = sum(es[1:layer + 1], jnp.zeros_like(mx)) / sum(es)

    z = proj_ref[:, OFF_FA:OFF_FA + A_WIDTH]
    sig = jax.nn.sigmoid(z)
    f = lb + (1.0 - lb) * sig
    logf = jnp.log(jnp.maximum(f, LOG_FLOOR)) * valid
    kk = (1.0 - lb) * (1.0 - sig) * valid
    proj_ref[:, OFF_QA:OFF_QA + A_WIDTH] = _silu(proj_ref[:, OFF_QA:OFF_QA + A_WIDTH])
    proj_ref[:, OFF_FA:OFF_FA + A_WIDTH] = kk
    lf_hi = logf.astype(BF16)
    lf_lo = (logf - lf_hi.astype(F32)).astype(BF16)
    mstack = mstack_ref[...]
    expo_ref[...] = _dot(mstack, lf_hi) + _dot(mstack, lf_lo)
    n_lv = len(LEVELS)

    for hd in range(HA):
        cs = slice(hd * DK_A, (hd + 1) * DK_A)
        qf_h = proj_ref[:, OFF_QA + hd * DK_A:OFF_QA + (hd + 1) * DK_A]
        kk_h = proj_ref[:, OFF_FA + hd * DK_A:OFF_FA + (hd + 1) * DK_A]
        v_h = proj_ref[:, OFF_IA + hd * DK_A:OFF_IA + (hd + 1) * DK_A].astype(BF16)
        attn = _dot_nt(qf_h.astype(BF16), kk_h.astype(BF16)) * lmask_ref[n_lv]
        for li, bsz in enumerate(LEVELS):
            upper = (rloc & (bsz - 1)) >= (bsz // 2)
            xb = (jnp.where(upper, qf_h, kk_h) * jnp.exp(expo_ref[li * TILE:(li + 1) * TILE, cs])).astype(BF16)
            attn = attn + _dot_nt(xb, xb) * lmask_ref[li]
        o = _dot(attn.astype(BF16), v_h)
        st = st_ref[hd]
        g_inc = expo_ref[n_lv * TILE:(n_lv + 1) * TILE, cs]
        q_dec = (qf_h * jnp.exp(g_inc)).astype(BF16)
        o = o + _dot_nt(q_dec, st.astype(BF16))
        k_dec = (kk_h * jnp.exp(expo_ref[(n_lv + 1) * TILE:(n_lv + 2) * TILE, cs])).astype(BF16)
        a_tot = jnp.exp(g_inc[TILE - 1:TILE])
        st_ref[hd] = a_tot * st + _dot_tn(v_h, k_dec)
        ya = o * lax.rsqrt(jnp.mean(o * o, axis=-1, keepdims=True) + RMS_EPS) * hg_ref[...]
        ga = proj_ref[:, OFF_GA + hd * DK_A:OFF_GA + (hd + 1) * DK_A]
        mixed_ref[:, hd * DK_A:(hd + 1) * DK_A] = (ya * _silu(ga)).astype(BF16)

    lane = lax.broadcasted_iota(jnp.int32, (1, LANES), 1)
    lo_half = lane < DH_B
    first = (lane & (DH_B - 1)) < DH_B // 2
    cos = cos_ref[...]
    sin = sin_ref[...]

    def norm_rope(x, g):
        sq = x * x
        s0 = jnp.sum(jnp.where(lo_half, sq, 0.0), axis=-1, keepdims=True)
        s1 = jnp.sum(jnp.where(lo_half, 0.0, sq), axis=-1, keepdims=True)
        msq = jnp.where(lo_half, s0, s1) * (1.0 / DH_B)
        y = x * lax.rsqrt(msq + RMS_EPS) * g
        rot = jnp.where(first, pltpu.roll(y, LANES - DH_B // 2, axis=1),
                        pltpu.roll(y, DH_B // 2, axis=1))
        return y * cos + rot * sin

    kc = norm_rope(proj_ref[:, OFF_KB:OFF_KB + KV_WIDTH], kg_ref[...])
    vc = proj_ref[:, OFF_VB:OFF_VB + KV_WIDTH]
    cur0 = N_META + TILE
    is_meta_tile = blk == 0
    for buf, val in ((kbuf_ref, kc), (ksw_ref, pltpu.roll(kc, DH_B, axis=1)),
                     (vbuf_ref, vc), (vsw_ref, pltpu.roll(vc, DH_B, axis=1))):
        val = val.astype(BF16)
        buf[cur0:cur0 + TILE, :] = val
        for m0 in (0, cur0 + TILE):
            buf[m0:m0 + N_META, :] = jnp.where(is_meta_tile, val[PAD_FRONT:], buf[m0:m0 + N_META, :])

    scale = DH_B ** -0.5
    qcols = []
    for j in range(HB // 2):
        qj = norm_rope(proj_ref[:, OFF_QB + j * LANES:OFF_QB + (j + 1) * LANES], qg_ref[...]) * scale
        qcols.append((jnp.where(lo_half, qj, 0.0).astype(BF16), jnp.where(lo_half, 0.0, qj).astype(BF16)))

    grp = HB // KVH_B
    heads_same = [hh for hh in range(HB) if (hh % 2) == (hh // grp)]
    heads_swap = [hh for hh in range(HB) if (hh % 2) != (hh // grp)]
    col = lax.broadcasted_iota(jnp.int32, (1, KWIN), 1)
    big = jnp.int32(1 << 30)
    for sb in range(TILE // QB):
        q0 = sb * QB
        w0 = sb * QB
        if sb == 0:
            is_meta = col < N_META
            kj = jnp.where(is_meta, PAD_FRONT + col,
                           jnp.where(col < cur0, (blk - 1) * TILE + col - N_META, blk * TILE + col - cur0))
        else:
            n_prev = cur0 - w0
            m_lo = n_prev + TILE
            is_meta = (col >= m_lo) & (col < m_lo + N_META)
            kj = jnp.where(col < n_prev, (blk - 1) * TILE + (TILE - n_prev) + col,
                           jnp.where(col < m_lo, blk * TILE + col - n_prev,
                                     jnp.where(is_meta, PAD_FRONT + col - m_lo, big)))
        qi = blk * TILE + q0 + lax.broadcasted_iota(jnp.int32, (QB, 1), 0)
        allowed = (kj <= qi) & (is_meta | ((kj > qi - TILE) & (kj >= TILE)))

        outs = {}
        for heads, k_ref, v_ref in ((heads_same, kbuf_ref, vbuf_ref), (heads_swap, ksw_ref, vsw_ref)):
            lhs = jnp.concatenate([qcols[hh // 2][hh % 2][q0:q0 + QB] for hh in heads], axis=0)
            s_all = _dot_nt(lhs, k_ref[w0:w0 + KWIN, :])
            ps = []
            for n, hh in enumerate(heads):
                s = jnp.where(allowed, s_all[n * QB:(n + 1) * QB], NEG_INF)
                sink = sinks_ref[hh]
                mxs = jnp.maximum(jnp.max(s, axis=-1, keepdims=True), sink)
                pr = jnp.exp(s - mxs)
                den = jnp.sum(pr, axis=-1, keepdims=True) + jnp.exp(sink - mxs)
                ps.append((pr * (1.0 / den)).astype(BF16))
            o_all = _dot(jnp.concatenate(ps, axis=0), v_ref[w0:w0 + KWIN, :])
            for n, hh in enumerate(heads):
                outs[hh] = o_all[n * QB:(n + 1) * QB]
        for j in range(HB // 2):
            yb = jnp.where(lo_half, outs[2 * j], outs[2 * j + 1])
            gb = proj_ref[q0:q0 + QB, OFF_GB + j * LANES:OFF_GB + (j + 1) * LANES]
            mixed_ref[q0:q0 + QB, A_WIDTH + j * LANES:A_WIDTH + (j + 1) * LANES] = (
                yb * _silu(gb)).astype(BF16)

    for buf in (kbuf_ref, ksw_ref, vbuf_ref, vsw_ref):
        buf[N_META:cur0, :] = buf[cur0:cur0 + TILE, :]

    wmax = max(POOL_WINDOWS)
    ug = proj_ref[:, OFF_UC:OFF_UC + C_WIDTH] * valid
    uext_ref[wmax:wmax + TILE, :] = ug
    n_valid = jnp.maximum(row - (PAD_FRONT - 1), 0)
    for g, w in enumerate(POOL_WINDOWS):
        cs = slice(g * CG_C, (g + 1) * CG_C)
        acc = uext_ref[wmax:wmax + TILE, cs]
        for s in range(1, w):
            acc = acc + uext_ref[wmax - s:wmax - s + TILE, cs]
        cnt = (n_valid - jnp.maximum(row - w - (PAD_FRONT - 1), 0)).astype(F32)
        pooled = (acc / jnp.maximum(cnt, 1.0) - ug[:, cs]) * valid
        y = _dot(pooled.astype(BF16), pw_ref[g]) * ps_ref[:, cs]
        gc = proj_ref[:, OFF_GC + g * CG_C:OFF_GC + (g + 1) * CG_C]
        mixed_ref[:, A_WIDTH + B_WIDTH + g * CG_C:A_WIDTH + B_WIDTH + (g + 1) * CG_C] = (
            y * _silu(gc)).astype(BF16)
    uext_ref[0:wmax, :] = ug[TILE - wmax:]


def _layer_kernel(layer, first, sinks_ref, *refs):
    n_h = 4 if first else 2
    h_refs, refs = refs[:n_h], refs[n_h:]
    (cos_ref, sin_ref, ng_ref, win_ref, lbl_ref, qg_ref, kg_ref, hg_ref, pw_ref, ps_ref, wout_ref,
     mstack_ref, lmask_ref, out_ref, proj0_ref, proj1_ref, mixed0_ref, mixed1_ref, expo_ref, st_ref,
     kbuf_ref, ksw_ref, vbuf_ref, vsw_ref, uext_ref) = refs
    t = pl.program_id(0)

    @pl.when(t == 0)
    def _zero_state():
        for ref in (proj0_ref, proj1_ref, mixed0_ref, mixed1_ref, st_ref, kbuf_ref, ksw_ref,
                    vbuf_ref, vsw_ref, uext_ref):
            ref[...] = jnp.zeros_like(ref)

    def tile_rows(tile_idx, x_ref, meta_ref=None):
        if not first:
            return x_ref[...]
        front = jnp.concatenate([jnp.zeros((PAD_FRONT, x_ref.shape[1]), F32), meta_ref[...]], axis=0)
        return jnp.where(tile_idx == 0, front, x_ref[...])

    def step(proj_in, proj_mix, mixed_mix, mixed_out):
        _mix_tile(layer, t - LAG_MIX, sinks_ref, cos_ref, sin_ref, lbl_ref, qg_ref, kg_ref, hg_ref,
                  pw_ref, ps_ref, mstack_ref, lmask_ref, proj_mix, mixed_mix, expo_ref, st_ref,
                  kbuf_ref, ksw_ref, vbuf_ref, vsw_ref, uext_ref)
        h = tile_rows(t, h_refs[0], h_refs[1] if first else None)
        ms = jnp.mean(h * h, axis=-1, keepdims=True)
        xn = (h * lax.rsqrt(ms + RMS_EPS) * ng_ref[...]).astype(BF16)
        for c0 in range(0, IN_COLS, PROJ_CHUNK):
            proj_in[:, c0:c0 + PROJ_CHUNK] = _dot(xn, win_ref[:, c0:c0 + PROJ_CHUNK])
        h_res = tile_rows(t - LAG_OUT, h_refs[-2] if first else h_refs[-1], h_refs[-1] if first else None)
        out_ref[...] = h_res + _dot(mixed_out[...], wout_ref[...])

    @pl.when(t % 2 == 0)
    def _even():
        step(proj0_ref, proj1_ref, mixed1_ref, mixed0_ref)

    @pl.when(t % 2 == 1)
    def _odd():
        step(proj1_ref, proj0_ref, mixed0_ref, mixed1_ref)


def _layer_call(layer, first, last, h_in, meta, cos, sin, sinks, norm_g, w_in, lb_logits, qg, kg, hg,
                pool_w, pool_scale, w_out, mstack, lmask):
    d = h_in.shape[1]
    n_tiles = cos.shape[0] // TILE

    def rows(width, lag, skip_front=False):
        off = lag + (1 if skip_front else 0)
        hi = n_tiles - 1 - (1 if skip_front else 0)
        return pl.BlockSpec((TILE, width), lambda t: (jnp.clip(t - off, 0, hi), 0))

    def whole(shape, single=False):
        idx = lambda t: (0,) * len(shape)
        if single:
            return pl.BlockSpec(shape, idx, pipeline_mode=pl.Buffered(1))
        return pl.BlockSpec(shape, idx)

    in_specs = [pl.BlockSpec(memory_space=pltpu.SMEM)]
    args = [sinks]
    for lag in (0, LAG_OUT):
        in_specs.append(rows(d, lag, skip_front=first))
        args.append(h_in)
        if first:
            in_specs.append(whole(meta.shape))
            args.append(meta)
    in_specs += [
        rows(LANES, LAG_MIX), rows(LANES, LAG_MIX),
        whole(norm_g.shape),
        whole(w_in.shape, single=True),
        whole(lb_logits.shape),
        whole(qg.shape), whole(kg.shape), whole(hg.shape),
        whole(pool_w.shape), whole(pool_scale.shape),
        whole(w_out.shape, single=True),
        whole(mstack.shape, single=True), whole(lmask.shape, single=True),
    ]
    args += [cos, sin, norm_g, w_in, lb_logits, qg, kg, hg, pool_w, pool_scale, w_out, mstack, lmask]
    out_rows = (n_tiles - 1) * TILE if last else n_tiles * TILE
    scratch = [
        pltpu.VMEM((TILE, IN_COLS), F32),
        pltpu.VMEM((TILE, IN_COLS), F32),
        pltpu.VMEM((TILE, MIX_WIDTH), BF16),
        pltpu.VMEM((TILE, MIX_WIDTH), BF16),
        pltpu.VMEM(((len(LEVELS) + 2) * TILE, A_WIDTH), F32),
        pltpu.VMEM((HA, DK_A, DK_A), F32),
        pltpu.VMEM((KBUF_ROWS, LANES), BF16),
        pltpu.VMEM((KBUF_ROWS, LANES), BF16),
        pltpu.VMEM((KBUF_ROWS, LANES), BF16),
        pltpu.VMEM((KBUF_ROWS, LANES), BF16),
        pltpu.VMEM((max(POOL_WINDOWS) + TILE, C_WIDTH), F32),
    ]
    return pl.pallas_call(
        functools.partial(_layer_kernel, layer, first),
        grid=(n_tiles + LAG_OUT,),
        in_specs=in_specs,
        out_specs=rows(d, LAG_OUT, skip_front=last),
        out_shape=jax.ShapeDtypeStruct((out_rows, d), F32),
        scratch_shapes=scratch,
        compiler_params=pltpu.CompilerParams(
            dimension_semantics=("arbitrary",), vmem_limit_bytes=VMEM_LIMIT),
        name=f"hybrid_layer{layer}",
    )(*args)


def kernel(x, meta_tokens, lb_logits, norm_g, w_in, q_norm_g, k_norm_g, attn_sinks, hgrn_norm_g,
           pool_w, pool_scale, w_out):
    b, seq, d = x.shape
    depth = w_in.shape[0]
    assert b == 1 and seq % TILE == 0
    assert w_in.shape[2] == IN_COLS and w_out.shape[1] == MIX_WIDTH
    assert meta_tokens.shape[0] == N_META

    p = TILE + seq
    pos = (jnp.arange(p) - PAD_FRONT).astype(F32)
    half = DH_B // 2
    inv = jnp.power(ROPE_THETA, -jnp.arange(half, dtype=F32) * 2.0 / DH_B)
    ang = pos[:, None] * inv[None, :]
    cos = jnp.tile(jnp.cos(ang), (1, LANES // half))
    sin = jnp.tile(jnp.concatenate([-jnp.sin(ang), jnp.sin(ang)], axis=1), (1, LANES // DH_B))

    mstack_np, lmask_np = _decay_constants()
    mstack = jnp.asarray(mstack_np, BF16)
    lmask = jnp.asarray(lmask_np, F32)

    h = x[0]
    for l in range(depth):
        h = _layer_call(
            l, l == 0, l == depth - 1, h, meta_tokens.astype(F32), cos, sin, attn_sinks[l],
            norm_g[l][None, :], w_in[l].astype(BF16), lb_logits,
            jnp.tile(q_norm_g[l], LANES // DH_B)[None, :], jnp.tile(k_norm_g[l], LANES // DH_B)[None, :],
            hgrn_norm_g[l][None, :], pool_w[l].astype(BF16), pool_scale[l][None, :],
            w_out[l].astype(BF16), mstack, lmask)
    return h[None]
```

```python
import functools

import numpy as np
import jax
import jax.numpy as jnp
from jax import lax
from jax.experimental import pallas as pl
from jax.experimental.pallas import tpu as pltpu

F32 = jnp.float32
BF16 = jnp.bfloat16

N_META = 16
TILE = 128
PAD_FRONT = TILE - N_META
STEP = 2 * TILE
RMS_EPS = 1e-6
NEG_INF = -1e30
LOG_FLOOR = 1e-30
ROPE_THETA = 10000.0

HA, DK_A = 4, 128
A_WIDTH = HA * DK_A
HB, KVH_B, DH_B = 16, 2, 64
B_WIDTH = HB * DH_B
KV_WIDTH = KVH_B * DH_B
POOL_WINDOWS = (2, 4, 8, 16)
CG_C = 128
C_WIDTH = len(POOL_WINDOWS) * CG_C
MIX_WIDTH = A_WIDTH + B_WIDTH + C_WIDTH

OFF_QA = 0
OFF_FA = OFF_QA + A_WIDTH
OFF_IA = OFF_FA + A_WIDTH
OFF_GA = OFF_IA + A_WIDTH
OFF_QB = OFF_GA + A_WIDTH
OFF_KB = OFF_QB + B_WIDTH
OFF_VB = OFF_KB + KV_WIDTH
OFF_GB = OFF_VB + KV_WIDTH
OFF_UC = OFF_GB + B_WIDTH
OFF_GC = OFF_UC + C_WIDTH
IN_COLS = OFF_GC + C_WIDTH
MM_CHUNK = 256

LANES = 128
QB = 64
KWIN = 256
KBUF_ROWS = 320
LEVELS = (128, 64, 32, 16, 8, 4, 2)
MIX_TICKS = 1 + HA + 2 + 2 * (TILE // QB) + len(POOL_WINDOWS)
VMEM_LIMIT = 60 * 1024 * 1024


def _decay_constants():
    r = np.arange(TILE)
    mats, masks = [], []
    for blk in LEVELS:
        half = blk // 2
        pos = r % blk
        piv = (r // blk) * blk + half - 1
        m = np.zeros((TILE, TILE), np.float32)
        for i in range(TILE):
            if pos[i] >= half:
                m[i, piv[i] + 1:i + 1] = 1.0
            else:
                m[i, i + 1:piv[i] + 1] = 1.0
        mats.append(m)
        same = (r[:, None] // blk) == (r[None, :] // blk)
        masks.append((same & (pos[:, None] >= half) & (pos[None, :] < half)).astype(np.float32))
    mats.append(np.tril(np.ones((TILE, TILE), np.float32)))
    mats.append(np.triu(np.ones((TILE, TILE), np.float32), 1))
    masks.append(np.eye(TILE, dtype=np.float32))
    return np.concatenate(mats, axis=0), np.stack(masks, axis=0)


def _dot(a, b):
    return jnp.dot(a, b, preferred_element_type=F32)


def _dot_nt(a, b):
    return lax.dot_general(a, b, (((1,), (1,)), ((), ())), preferred_element_type=F32)


def _dot_tn(a, b):
    return lax.dot_general(a, b, (((0,), (0,)), ((), ())), preferred_element_type=F32)


def _silu(x):
    return x * jax.nn.sigmoid(x)


def _mix_tile(layer, blk, sinks_ref, cos_ref, sin_ref, lbl_ref, qg_ref, kg_ref, hg_ref,
              pw_ref, ps_ref, mstack_ref, lmask_ref, proj_ref, mixed_ref, expo_ref, st_ref,
              kbuf_ref, ksw_ref, vbuf_ref, vsw_ref, uext_ref, tick):
    rloc = lax.broadcasted_iota(jnp.int32, (TILE, 1), 0)
    row = blk * TILE + rloc
    valid = (row >= PAD_FRONT).astype(F32)

    rows = [lbl_ref[j:j + 1, :] for j in range(lbl_ref.shape[0])]
    mx = functools.reduce(jnp.maximum, rows)
    es = [jnp.exp(x - mx) for x in rows]
    lb = sum(es[1:layer + 1], jnp.zeros_like(mx)) / sum(es)

    z = proj_ref[:, OFF_FA:OFF_FA + A_WIDTH]
    sig = jax.nn.sigmoid(z)
    f = lb + (1.0 - lb) * sig
    logf = jnp.log(jnp.maximum(f, LOG_FLOOR)) * valid
    kk = (1.0 - lb) * (1.0 - sig) * valid
    proj_ref[:, OFF_QA:OFF_QA + A_WIDTH] = _silu(proj_ref[:, OFF_QA:OFF_QA + A_WIDTH])
    proj_ref[:, OFF_FA:OFF_FA + A_WIDTH] = kk
    lf_hi = logf.astype(BF16)
    lf_lo = (logf - lf_hi.astype(F32)).astype(BF16)
    mstack = mstack_ref[...]
    expo_ref[...] = _dot(mstack, lf_hi) + _dot(mstack, lf_lo)
    n_lv = len(LEVELS)
    tick()

    for hd in range(HA):
        cs = slice(hd * DK_A, (hd + 1) * DK_A)
        qf_h = proj_ref[:, OFF_QA + hd * DK_A:OFF_QA + (hd + 1) * DK_A]
        kk_h = proj_ref[:, OFF_FA + hd * DK_A:OFF_FA + (hd + 1) * DK_A]
        v_h = proj_ref[:, OFF_IA + hd * DK_A:OFF_IA + (hd + 1) * DK_A].astype(BF16)
        attn = _dot_nt(qf_h.astype(BF16), kk_h.astype(BF16)) * lmask_ref[n_lv]
        for li, bsz in enumerate(LEVELS):
            upper = (rloc & (bsz - 1)) >= (bsz // 2)
            xb = (jnp.where(upper, qf_h, kk_h) * jnp.exp(expo_ref[li * TILE:(li + 1) * TILE, cs])).astype(BF16)
            attn = attn + _dot_nt(xb, xb) * lmask_ref[li]
        o = _dot(attn.astype(BF16), v_h)
        st = st_ref[hd]
        g_inc = expo_ref[n_lv * TILE:(n_lv + 1) * TILE, cs]
        q_dec = (qf_h * jnp.exp(g_inc)).astype(BF16)
        o = o + _dot_nt(q_dec, st.astype(BF16))
        k_dec = (kk_h * jnp.exp(expo_ref[(n_lv + 1) * TILE:(n_lv + 2) * TILE, cs])).astype(BF16)
        a_tot = jnp.exp(g_inc[TILE - 1:TILE])
        st_ref[hd] = a_tot * st + _dot_tn(v_h, k_dec)
        ya = o * lax.rsqrt(jnp.mean(o * o, axis=-1, keepdims=True) + RMS_EPS) * hg_ref[...]
        ga = proj_ref[:, OFF_GA + hd * DK_A:OFF_GA + (hd + 1) * DK_A]
        mixed_ref[:, hd * DK_A:(hd + 1) * DK_A] = (ya * _silu(ga)).astype(BF16)
        tick()

    lane = lax.broadcasted_iota(jnp.int32, (1, LANES), 1)
    lo_half = lane < DH_B
    first = (lane & (DH_B - 1)) < DH_B // 2
    cos = cos_ref[...]
    sin = sin_ref[...]

    def norm_rope(x, g):
        sq = x * x
        s0 = jnp.sum(jnp.where(lo_half, sq, 0.0), axis=-1, keepdims=True)
        s1 = jnp.sum(jnp.where(lo_half, 0.0, sq), axis=-1, keepdims=True)
        msq = jnp.where(lo_half, s0, s1) * (1.0 / DH_B)
        y = x * lax.rsqrt(msq + RMS_EPS) * g
        rot = jnp.where(first, pltpu.roll(y, LANES - DH_B // 2, axis=1),
                        pltpu.roll(y, DH_B // 2, axis=1))
        return y * cos + rot * sin

    kc = norm_rope(proj_ref[:, OFF_KB:OFF_KB + KV_WIDTH], kg_ref[...])
    vc = proj_ref[:, OFF_VB:OFF_VB + KV_WIDTH]
    cur0 = N_META + TILE
    is_meta_tile = blk == 0
    for buf, val in ((kbuf_ref, kc), (ksw_ref, pltpu.roll(kc, DH_B, axis=1)),
                     (vbuf_ref, vc), (vsw_ref, pltpu.roll(vc, DH_B, axis=1))):
        val = val.astype(BF16)
        buf[cur0:cur0 + TILE, :] = val
        for m0 in (0, cur0 + TILE):
            buf[m0:m0 + N_META, :] = jnp.where(is_meta_tile, val[PAD_FRONT:], buf[m0:m0 + N_META, :])
    tick()

    scale = DH_B ** -0.5
    qcols = []
    for j in range(HB // 2):
        qj = norm_rope(proj_ref[:, OFF_QB + j * LANES:OFF_QB + (j + 1) * LANES], qg_ref[...]) * scale
        qcols.append((jnp.where(lo_half, qj, 0.0).astype(BF16), jnp.where(lo_half, 0.0, qj).astype(BF16)))
    tick()

    grp = HB // KVH_B
    heads_same = [hh for hh in range(HB) if (hh % 2) == (hh // grp)]
    heads_swap = [hh for hh in range(HB) if (hh % 2) != (hh // grp)]
    col = lax.broadcasted_iota(jnp.int32, (1, KWIN), 1)
    big = jnp.int32(1 << 30)
    for sb in range(TILE // QB):
        q0 = sb * QB
        w0 = sb * QB
        if sb == 0:
            is_meta = col < N_META
            kj = jnp.where(is_meta, PAD_FRONT + col,
                           jnp.where(col < cur0, (blk - 1) * TILE + col - N_META, blk * TILE + col - cur0))
        else:
            n_prev = cur0 - w0
            m_lo = n_prev + TILE
            is_meta = (col >= m_lo) & (col < m_lo + N_META)
            kj = jnp.where(col < n_prev, (blk - 1) * TILE + (TILE - n_prev) + col,
                           jnp.where(col < m_lo, blk * TILE + col - n_prev,
                                     jnp.where(is_meta, PAD_FRONT + col - m_lo, big)))
        qi = blk * TILE + q0 + lax.broadcasted_iota(jnp.int32, (QB, 1), 0)
        allowed = (kj <= qi) & (is_meta | ((kj > qi - TILE) & (kj >= TILE)))

        outs = {}
        for heads, k_ref, v_ref in ((heads_same, kbuf_ref, vbuf_ref), (heads_swap, ksw_ref, vsw_ref)):
            lhs = jnp.concatenate([qcols[hh // 2][hh % 2][q0:q0 + QB] for hh in heads], axis=0)
            s_all = _dot_nt(lhs, k_ref[w0:w0 + KWIN, :])
            ps = []
            for n, hh in enumerate(heads):
                s = jnp.where(allowed, s_all[n * QB:(n + 1) * QB], NEG_INF)
                sink = sinks_ref[hh]
                mxs = jnp.maximum(jnp.max(s, axis=-1, keepdims=True), sink)
                pr = jnp.exp(s - mxs)
                den = jnp.sum(pr, axis=-1, keepdims=True) + jnp.exp(sink - mxs)
                ps.append((pr * (1.0 / den)).astype(BF16))
            o_all = _dot(jnp.concatenate(ps, axis=0), v_ref[w0:w0 + KWIN, :])
            for n, hh in enumerate(heads):
                outs[hh] = o_all[n * QB:(n + 1) * QB]
            tick()
        for j in range(HB // 2):
            yb = jnp.where(lo_half, outs[2 * j], outs[2 * j + 1])
            gb = proj_ref[q0:q0 + QB, OFF_GB + j * LANES:OFF_GB + (j + 1) * LANES]
            mixed_ref[q0:q0 + QB, A_WIDTH + j * LANES:A_WIDTH + (j + 1) * LANES] = (
                yb * _silu(gb)).astype(BF16)

    for buf in (kbuf_ref, ksw_ref, vbuf_ref, vsw_ref):
        buf[N_META:cur0, :] = buf[cur0:cur0 + TILE, :]

    wmax = max(POOL_WINDOWS)
    ug = proj_ref[:, OFF_UC:OFF_UC + C_WIDTH] * valid
    uext_ref[wmax:wmax + TILE, :] = ug
    n_valid = jnp.maximum(row - (PAD_FRONT - 1), 0)
    for g, w in enumerate(POOL_WINDOWS):
        cs = slice(g * CG_C, (g + 1) * CG_C)
        acc = uext_ref[wmax:wmax + TILE, cs]
        for s in range(1, w):
            acc = acc + uext_ref[wmax - s:wmax - s + TILE, cs]
        cnt = (n_valid - jnp.maximum(row - w - (PAD_FRONT - 1), 0)).astype(F32)
        pooled = (acc / jnp.maximum(cnt, 1.0) - ug[:, cs]) * valid
        y = _dot(pooled.astype(BF16), pw_ref[g]) * ps_ref[:, cs]
        gc = proj_ref[:, OFF_GC + g * CG_C:OFF_GC + (g + 1) * CG_C]
        mixed_ref[:, A_WIDTH + B_WIDTH + g * CG_C:A_WIDTH + B_WIDTH + (g + 1) * CG_C] = (
            y * _silu(gc)).astype(BF16)
        tick()
    uext_ref[0:wmax, :] = ug[TILE - wmax:]


def _layer_kernel(layer, first, sinks_ref, *refs):
    n_h = 4 if first else 2
    h_refs, refs = refs[:n_h], refs[n_h:]
    (cosb_ref, sinb_ref, cosa_ref, sina_ref, ng_ref, win_ref, lbl_ref, qg_ref, kg_ref, hg_ref, pw_ref,
     ps_ref, wout_ref, mstack_ref, lmask_ref, out_ref, xn_ref, proja_ref, projb_ref, mixeda_ref,
     mixedb_ref, expo_ref, st_ref, kbuf_ref, ksw_ref, vbuf_ref, vsw_ref, uext_ref) = refs
    t = pl.program_id(0)

    @pl.when(t == 0)
    def _zero_state():
        for ref in (proja_ref, projb_ref, mixeda_ref, mixedb_ref, st_ref, kbuf_ref, ksw_ref,
                    vbuf_ref, vsw_ref, uext_ref):
            ref[...] = jnp.zeros_like(ref)

    def step_rows(step_idx, rws, cols, x_ref, meta_ref=None):
        x = x_ref[rws, cols]
        if not first:
            return x
        if rws.stop is not None and rws.stop < STEP:
            return jnp.where(step_idx == 0, 0.0, x)
        n_rows = x.shape[0]
        front = jnp.concatenate([jnp.zeros((n_rows - N_META, x.shape[1]), F32), meta_ref[:, cols]], axis=0)
        return jnp.where(step_idx == 0, front, x)

    d_model = out_ref.shape[1]

    h = step_rows(t, slice(None), slice(None), *h_refs[:n_h // 2])
    ms = jnp.mean(h * h, axis=-1, keepdims=True)
    xn_ref[...] = (h * lax.rsqrt(ms + RMS_EPS) * ng_ref[...]).astype(BF16)

    def half_step(rws_in, proj_in, blk_mix, trig, proj_mix, mixed_mix, rws_out, mixed_out):
        def out_chunk(c0):
            cols = slice(c0, c0 + MM_CHUNK)
            h_res = step_rows(t - 1, rws_out, cols, *h_refs[n_h // 2:])
            out_ref[rws_out, cols] = h_res + _dot(mixed_out[...], wout_ref[:, cols])

        def in_chunk(c0):
            cols = slice(c0, c0 + MM_CHUNK)
            proj_in[:, cols] = _dot(xn_ref[rws_in, :], win_ref[:, cols])

        work = [functools.partial(out_chunk, c0) for c0 in range(0, d_model, MM_CHUNK)]
        work += [functools.partial(in_chunk, c0) for c0 in range(0, IN_COLS, MM_CHUNK)]
        progress = {"ticks": 0, "done": 0}

        def tick():
            progress["ticks"] += 1
            target = min(len(work), -(-len(work) * progress["ticks"] // MIX_TICKS))
            while progress["done"] < target:
                work[progress["done"]]()
                progress["done"] += 1

        _mix_tile(layer, blk_mix, sinks_ref, trig[0], trig[1], lbl_ref, qg_ref, kg_ref, hg_ref,
                  pw_ref, ps_ref, mstack_ref, lmask_ref, proj_mix, mixed_mix, expo_ref, st_ref,
                  kbuf_ref, ksw_ref, vbuf_ref, vsw_ref, uext_ref, tick)
        assert progress["done"] == len(work), "MIX_TICKS must match the tick() calls in _mix_tile"

    rows_a, rows_b = slice(0, TILE), slice(TILE, STEP)
    half_step(rows_a, proja_ref, 2 * t - 2, (cosb_ref, sinb_ref), projb_ref, mixedb_ref, rows_a, mixeda_ref)
    half_step(rows_b, projb_ref, 2 * t - 1, (cosa_ref, sina_ref), proja_ref, mixeda_ref, rows_b, mixedb_ref)


def _layer_call(layer, first, last, h_in, meta, cos, sin, sinks, norm_g, w_in, lb_logits, qg, kg, hg,
                pool_w, pool_scale, w_out, mstack, lmask):
    d = h_in.shape[1]
    n_steps = cos.shape[0] // STEP

    def rows(width, lag, skip_front=False):
        off = lag + (1 if skip_front else 0)
        hi = n_steps - 1 - (1 if skip_front else 0)
        return pl.BlockSpec((STEP, width), lambda t: (jnp.clip(t - off, 0, hi), 0))

    def tile_rows(width, tile_off):
        return pl.BlockSpec((TILE, width), lambda t: (jnp.clip(2 * t + tile_off, 0, 2 * n_steps - 1), 0))

    def whole(shape, single=False):
        idx = lambda t: (0,) * len(shape)
        if single:
            return pl.BlockSpec(shape, idx, pipeline_mode=pl.Buffered(1))
        return pl.BlockSpec(shape, idx)

    in_specs = [pl.BlockSpec(memory_space=pltpu.SMEM)]
    args = [sinks]
    for lag in (0, 1):
        in_specs.append(rows(d, lag, skip_front=first))
        args.append(h_in)
        if first:
            in_specs.append(whole(meta.shape))
            args.append(meta)
    in_specs += [
        tile_rows(LANES, -1), tile_rows(LANES, -1),
        tile_rows(LANES, 0), tile_rows(LANES, 0),
        whole(norm_g.shape),
        whole(w_in.shape, single=True),
        whole(lb_logits.shape),
        whole(qg.shape), whole(kg.shape), whole(hg.shape),
        whole(pool_w.shape), whole(pool_scale.shape),
        whole(w_out.shape, single=True),
        whole(mstack.shape, single=True), whole(lmask.shape, single=True),
    ]
    args += [cos, sin, cos, sin, norm_g, w_in, lb_logits, qg, kg, hg, pool_w, pool_scale, w_out, mstack, lmask]
    out_rows = (n_steps - 1) * STEP if last else n_steps * STEP
    scratch = [
        pltpu.VMEM((STEP, d), BF16),
        pltpu.VMEM((TILE, IN_COLS), F32),
        pltpu.VMEM((TILE, IN_COLS), F32),
        pltpu.VMEM((TILE, MIX_WIDTH), BF16),
        pltpu.VMEM((TILE, MIX_WIDTH), BF16),
        pltpu.VMEM(((len(LEVELS) + 2) * TILE, A_WIDTH), F32),
        pltpu.VMEM((HA, DK_A, DK_A), F32),
        pltpu.VMEM((KBUF_ROWS, LANES), BF16),
        pltpu.VMEM((KBUF_ROWS, LANES), BF16),
        pltpu.VMEM((KBUF_ROWS, LANES), BF16),
        pltpu.VMEM((KBUF_ROWS, LANES), BF16),
        pltpu.VMEM((max(POOL_WINDOWS) + TILE, C_WIDTH), F32),
    ]
    return pl.pallas_call(
        functools.partial(_layer_kernel, layer, first),
        grid=(n_steps + 1,),
        in_specs=in_specs,
        out_specs=rows(d, 1, skip_front=last),
        out_shape=jax.ShapeDtypeStruct((out_rows, d), F32),
        scratch_shapes=scratch,
        compiler_params=pltpu.CompilerParams(
            dimension_semantics=("arbitrary",), vmem_limit_bytes=VMEM_LIMIT),
        name=f"hybrid_layer{layer}",
    )(*args)


def kernel(x, meta_tokens, lb_logits, norm_g, w_in, q_norm_g, k_norm_g, attn_sinks, hgrn_norm_g,
           pool_w, pool_scale, w_out):
    b, seq, d = x.shape
    depth = w_in.shape[0]
    assert b == 1 and seq % STEP == 0
    assert w_in.shape[2] == IN_COLS and w_out.shape[1] == MIX_WIDTH
    assert meta_tokens.shape[0] == N_META

    p = STEP + seq
    pos = (jnp.arange(p) - (STEP - N_META)).astype(F32)
    half = DH_B // 2
    inv = jnp.power(ROPE_THETA, -jnp.arange(half, dtype=F32) * 2.0 / DH_B)
    ang = pos[:, None] * inv[None, :]
    cos = jnp.tile(jnp.cos(ang), (1, LANES // half))
    sin = jnp.tile(jnp.concatenate([-jnp.sin(ang), jnp.sin(ang)], axis=1), (1, LANES // DH_B))

    mstack_np, lmask_np = _decay_constants()
    mstack = jnp.asarray(mstack_np, BF16)
    lmask = jnp.asarray(lmask_np, F32)

    h = x[0]
    for l in range(depth):
        h = _layer_call(
            l, l == 0, l == depth - 1, h, meta_tokens.astype(F32), cos, sin, attn_sinks[l],
            norm_g[l][None, :], w_in[l].astype(BF16), lb_logits,
            jnp.tile(q_norm_g[l], LANES // DH_B)[None, :], jnp.tile(k_norm_g[l], LANES // DH_B)[None, :],
            hgrn_norm_g[l][None, :], pool_w[l].astype(BF16), pool_scale[l][None, :],
            w_out[l].astype(BF16), mstack, lmask)
    return h[None]
```

```python
import functools

import numpy as np
import jax
import jax.numpy as jnp
from jax import lax
from jax.experimental import pallas as pl
from jax.experimental.pallas import tpu as pltpu

F32 = jnp.float32
BF16 = jnp.bfloat16

N_META = 16
TILE = 128
PAD_FRONT = TILE - N_META
STEP = 2 * TILE
RMS_EPS = 1e-6
NEG_INF = -1e30
LOG_FLOOR = 1e-30
ROPE_THETA = 10000.0

HA, DK_A = 4, 128
A_WIDTH = HA * DK_A
HB, KVH_B, DH_B = 16, 2, 64
B_WIDTH = HB * DH_B
KV_WIDTH = KVH_B * DH_B
POOL_WINDOWS = (2, 4, 8, 16)
CG_C = 128
C_WIDTH = len(POOL_WINDOWS) * CG_C
MIX_WIDTH = A_WIDTH + B_WIDTH + C_WIDTH

OFF_QA = 0
OFF_FA = OFF_QA + A_WIDTH
OFF_IA = OFF_FA + A_WIDTH
OFF_GA = OFF_IA + A_WIDTH
OFF_QB = OFF_GA + A_WIDTH
OFF_KB = OFF_QB + B_WIDTH
OFF_VB = OFF_KB + KV_WIDTH
OFF_GB = OFF_VB + KV_WIDTH
OFF_UC = OFF_GB + B_WIDTH
OFF_GC = OFF_UC + C_WIDTH
IN_COLS = OFF_GC + C_WIDTH
MM_CHUNK = 256

LANES = 128
QB = 64
KWIN = 256
KBUF_ROWS = 320
LEVELS = (128, 64, 32, 16, 8, 4, 2)
MIX_TICKS = 1 + HA + 2 + 2 * (TILE // QB) + len(POOL_WINDOWS)
VMEM_LIMIT = 60 * 1024 * 1024


def _decay_constants():
    r = np.arange(TILE)
    mats, masks = [], []
    for blk in LEVELS:
        half = blk // 2
        pos = r % blk
        piv = (r // blk) * blk + half - 1
        m = np.zeros((TILE, TILE), np.float32)
        for i in range(TILE):
            if pos[i] >= half:
                m[i, piv[i] + 1:i + 1] = 1.0
            else:
                m[i, i + 1:piv[i] + 1] = 1.0
        mats.append(m)
        same = (r[:, None] // blk) == (r[None, :] // blk)
        masks.append((same & (pos[:, None] >= half) & (pos[None, :] < half)).astype(np.float32))
    mats.append(np.tril(np.ones((TILE, TILE), np.float32)))
    mats.append(np.triu(np.ones((TILE, TILE), np.float32), 1))
    masks.append(np.eye(TILE, dtype=np.float32))
    return np.concatenate(mats, axis=0), np.stack(masks, axis=0)


def _dot(a, b):
    return jnp.dot(a, b, preferred_element_type=F32)


def _dot_nt(a, b):
    return lax.dot_general(a, b, (((1,), (1,)), ((), ())), preferred_element_type=F32)


def _dot_tn(a, b):
    return lax.dot_general(a, b, (((0,), (0,)), ((), ())), preferred_element_type=F32)


def _silu(x):
    return x * jax.nn.sigmoid(x)


def _mix_tile(layer, blk, sinks_ref, cos_ref, sin_ref, lbl_ref, qg_ref, kg_ref, hg_ref,
              pw_ref, ps_ref, mstack_ref, lmask_ref, proj_ref, mixed_ref, expo_ref, st_ref,
              kbuf_ref, ksw_ref, vbuf_ref, vsw_ref, uext_ref, tick):
    rloc = lax.broadcasted_iota(jnp.int32, (TILE, 1), 0)
    row = blk * TILE + rloc
    valid = (row >= PAD_FRONT).astype(F32)

    rows = [lbl_ref[j:j + 1, :] for j in range(lbl_ref.shape[0])]
    mx = functools.reduce(jnp.maximum, rows)
    es = [jnp.exp(x - mx) for x in rows]
    lb = sum(es[1:layer + 1], jnp.zeros_like(mx)) / sum(es)

    z = proj_ref[:, OFF_FA:OFF_FA + A_WIDTH]
    sig = jax.nn.sigmoid(z)
    f = lb + (1.0 - lb) * sig
    logf = jnp.log(jnp.maximum(f, LOG_FLOOR)) * valid
    kk = (1.0 - lb) * (1.0 - sig) * valid
    proj_ref[:, OFF_QA:OFF_QA + A_WIDTH] = _silu(proj_ref[:, OFF_QA:OFF_QA + A_WIDTH])
    proj_ref[:, OFF_FA:OFF_FA + A_WIDTH] = kk
    lf_hi = logf.astype(BF16)
    lf_lo = (logf - lf_hi.astype(F32)).astype(BF16)
    mstack = mstack_ref[...]
    expo_ref[...] = _dot(mstack, lf_hi) + _dot(mstack, lf_lo)
    n_lv = len(LEVELS)
    tick()

    for hd in range(HA):
        cs = slice(hd * DK_A, (hd + 1) * DK_A)
        qf_h = proj_ref[:, OFF_QA + hd * DK_A:OFF_QA + (hd + 1) * DK_A]
        kk_h = proj_ref[:, OFF_FA + hd * DK_A:OFF_FA + (hd + 1) * DK_A]
        v_h = proj_ref[:, OFF_IA + hd * DK_A:OFF_IA + (hd + 1) * DK_A].astype(BF16)
        attn = _dot_nt(qf_h.astype(BF16), kk_h.astype(BF16)) * lmask_ref[n_lv]
        for li, bsz in enumerate(LEVELS):
            upper = (rloc & (bsz - 1)) >= (bsz // 2)
            xb = (jnp.where(upper, qf_h, kk_h) * jnp.exp(expo_ref[li * TILE:(li + 1) * TILE, cs])).astype(BF16)
            attn = attn + _dot_nt(xb, xb) * lmask_ref[li]
        o = _dot(attn.astype(BF16), v_h)
        st = st_ref[hd]
        g_inc = expo_ref[n_lv * TILE:(n_lv + 1) * TILE, cs]
        q_dec = (qf_h * jnp.exp(g_inc)).astype(BF16)
        o = o + _dot_nt(q_dec, st.astype(BF16))
        k_dec = (kk_h * jnp.exp(expo_ref[(n_lv + 1) * TILE:(n_lv + 2) * TILE, cs])).astype(BF16)
        a_tot = jnp.exp(g_inc[TILE - 1:TILE])
        st_ref[hd] = a_tot * st + _dot_tn(v_h, k_dec)
        ya = o * lax.rsqrt(jnp.mean(o * o, axis=-1, keepdims=True) + RMS_EPS) * hg_ref[...]
        ga = proj_ref[:, OFF_GA + hd * DK_A:OFF_GA + (hd + 1) * DK_A]
        mixed_ref[:, hd * DK_A:(hd + 1) * DK_A] = (ya * _silu(ga)).astype(BF16)
        tick()

    lane = lax.broadcasted_iota(jnp.int32, (1, LANES), 1)
    lo_half = lane < DH_B
    first = (lane & (DH_B - 1)) < DH_B // 2
    cos = cos_ref[...]
    sin = sin_ref[...]

    def norm_rope(x, g):
        sq = x * x
        s0 = jnp.sum(jnp.where(lo_half, sq, 0.0), axis=-1, keepdims=True)
        s1 = jnp.sum(jnp.where(lo_half, 0.0, sq), axis=-1, keepdims=True)
        msq = jnp.where(lo_half, s0, s1) * (1.0 / DH_B)
        y = x * lax.rsqrt(msq + RMS_EPS) * g
        rot = jnp.where(first, pltpu.roll(y, LANES - DH_B // 2, axis=1),
                        pltpu.roll(y, DH_B // 2, axis=1))
        return y * cos + rot * sin

    kc = norm_rope(proj_ref[:, OFF_KB:OFF_KB + KV_WIDTH], kg_ref[...])
    vc = proj_ref[:, OFF_VB:OFF_VB + KV_WIDTH]
    cur0 = N_META + TILE
    is_meta_tile = blk == 0
    for buf, val in ((kbuf_ref, kc), (ksw_ref, pltpu.roll(kc, DH_B, axis=1)),
                     (vbuf_ref, vc), (vsw_ref, pltpu.roll(vc, DH_B, axis=1))):
        val = val.astype(BF16)
        buf[cur0:cur0 + TILE, :] = val
        for m0 in (0, cur0 + TILE):
            buf[m0:m0 + N_META, :] = jnp.where(is_meta_tile, val[PAD_FRONT:], buf[m0:m0 + N_META, :])
    tick()

    scale = DH_B ** -0.5
    qcols = []
    for j in range(HB // 2):
        qj = norm_rope(proj_ref[:, OFF_QB + j * LANES:OFF_QB + (j + 1) * LANES], qg_ref[...]) * scale
        qcols.append((jnp.where(lo_half, qj, 0.0).astype(BF16), jnp.where(lo_half, 0.0, qj).astype(BF16)))
    tick()

    grp = HB // KVH_B
    heads_same = [hh for hh in range(HB) if (hh % 2) == (hh // grp)]
    heads_swap = [hh for hh in range(HB) if (hh % 2) != (hh // grp)]
    col = lax.broadcasted_iota(jnp.int32, (1, KWIN), 1)
    big = jnp.int32(1 << 30)
    for sb in range(TILE // QB):
        q0 = sb * QB
        w0 = sb * QB
        if sb == 0:
            is_meta = col < N_META
            kj = jnp.where(is_meta, PAD_FRONT + col,
                           jnp.where(col < cur0, (blk - 1) * TILE + col - N_META, blk * TILE + col - cur0))
        else:
            n_prev = cur0 - w0
            m_lo = n_prev + TILE
            is_meta = (col >= m_lo) & (col < m_lo + N_META)
            kj = jnp.where(col < n_prev, (blk - 1) * TILE + (TILE - n_prev) + col,
                           jnp.where(col < m_lo, blk * TILE + col - n_prev,
                                     jnp.where(is_meta, PAD_FRONT + col - m_lo, big)))
        qi = blk * TILE + q0 + lax.broadcasted_iota(jnp.int32, (QB, 1), 0)
        allowed = (kj <= qi) & (is_meta | ((kj > qi - TILE) & (kj >= TILE)))

        outs = {}
        for heads, k_ref, v_ref in ((heads_same, kbuf_ref, vbuf_ref), (heads_swap, ksw_ref, vsw_ref)):
            lhs = jnp.concatenate([qcols[hh // 2][hh % 2][q0:q0 + QB] for hh in heads], axis=0)
            s_all = _dot_nt(lhs, k_ref[w0:w0 + KWIN, :])
            ps = []
            for n, hh in enumerate(heads):
                s = jnp.where(allowed, s_all[n * QB:(n + 1) * QB], NEG_INF)
                sink = sinks_ref[hh]
                mxs = jnp.maximum(jnp.max(s, axis=-1, keepdims=True), sink)
                pr = jnp.exp(s - mxs)
                den = jnp.sum(pr, axis=-1, keepdims=True) + jnp.exp(sink - mxs)
                ps.append((pr * (1.0 / den)).astype(BF16))
            o_all = _dot(jnp.concatenate(ps, axis=0), v_ref[w0:w0 + KWIN, :])
            for n, hh in enumerate(heads):
                outs[hh] = o_all[n * QB:(n + 1) * QB]
            tick()
        for j in range(HB // 2):
            yb = jnp.where(lo_half, outs[2 * j], outs[2 * j + 1])
            gb = proj_ref[q0:q0 + QB, OFF_GB + j * LANES:OFF_GB + (j + 1) * LANES]
            mixed_ref[q0:q0 + QB, A_WIDTH + j * LANES:A_WIDTH + (j + 1) * LANES] = (
                yb * _silu(gb)).astype(BF16)

    for buf in (kbuf_ref, ksw_ref, vbuf_ref, vsw_ref):
        buf[N_META:cur0, :] = buf[cur0:cur0 + TILE, :]

    wmax = max(POOL_WINDOWS)
    ug = proj_ref[:, OFF_UC:OFF_UC + C_WIDTH] * valid
    uext_ref[wmax:wmax + TILE, :] = ug
    n_valid = jnp.maximum(row - (PAD_FRONT - 1), 0)
    for g, w in enumerate(POOL_WINDOWS):
        cs = slice(g * CG_C, (g + 1) * CG_C)
        acc = uext_ref[wmax:wmax + TILE, cs]
        for s in range(1, w):
            acc = acc + uext_ref[wmax - s:wmax - s + TILE, cs]
        cnt = (n_valid - jnp.maximum(row - w - (PAD_FRONT - 1), 0)).astype(F32)
        pooled = (acc / jnp.maximum(cnt, 1.0) - ug[:, cs]) * valid
        y = _dot(pooled.astype(BF16), pw_ref[g]) * ps_ref[:, cs]
        gc = proj_ref[:, OFF_GC + g * CG_C:OFF_GC + (g + 1) * CG_C]
        mixed_ref[:, A_WIDTH + B_WIDTH + g * CG_C:A_WIDTH + B_WIDTH + (g + 1) * CG_C] = (
            y * _silu(gc)).astype(BF16)
        tick()
    uext_ref[0:wmax, :] = ug[TILE - wmax:]


def _layer_kernel(layer, first, sinks_ref, *refs):
    n_h = 4 if first else 2
    h_refs, refs = refs[:n_h], refs[n_h:]
    (cosb_ref, sinb_ref, cosa_ref, sina_ref, ng_ref, win_hbm, lbl_ref, qg_ref, kg_ref, hg_ref, pw_ref,
     ps_ref, wout_hbm, mstack_ref, lmask_ref, out_ref, win_ref, wout_ref, wsem, xn_ref, proja_ref,
     projb_ref, mixeda_ref, mixedb_ref, expo_ref, st_ref, kbuf_ref, ksw_ref, vbuf_ref, vsw_ref,
     uext_ref) = refs
    t = pl.program_id(0)

    @pl.when(t == 0)
    def _load_weights_and_zero_state():
        stage = (proja_ref, projb_ref)
        for w_hbm, w_vmem in ((win_hbm, win_ref), (wout_hbm, wout_ref)):
            width = w_vmem.shape[1]
            n_chunks = w_vmem.shape[0] // TILE

            def chunk_copy(k, slot, w_hbm=w_hbm, width=width):
                return pltpu.make_async_copy(w_hbm.at[layer, pl.ds(k * TILE, TILE), :],
                                             stage[slot].at[:, 0:width], wsem.at[slot])

            chunk_copy(0, 0).start()
            chunk_copy(1, 1).start()

            def pair(i, carry, w_vmem=w_vmem, width=width, n_chunks=n_chunks, chunk_copy=chunk_copy):
                for slot in (0, 1):
                    k = 2 * i + slot
                    chunk_copy(k, slot).wait()
                    w_vmem[pl.ds(pl.multiple_of(k * TILE, TILE), TILE), :] = stage[slot][:, 0:width].astype(BF16)

                    @pl.when(k + 2 < n_chunks)
                    def _next():
                        chunk_copy(k + 2, slot).start()
                return carry

            lax.fori_loop(0, n_chunks // 2, pair, 0)

        for ref in (proja_ref, projb_ref, mixeda_ref, mixedb_ref, st_ref, kbuf_ref, ksw_ref,
                    vbuf_ref, vsw_ref, uext_ref):
            ref[...] = jnp.zeros_like(ref)

    def step_rows(step_idx, rws, cols, x_ref, meta_ref=None):
        x = x_ref[rws, cols]
        if not first:
            return x
        if rws.stop is not None and rws.stop < STEP:
            return jnp.where(step_idx == 0, 0.0, x)
        n_rows = x.shape[0]
        front = jnp.concatenate([jnp.zeros((n_rows - N_META, x.shape[1]), F32), meta_ref[:, cols]], axis=0)
        return jnp.where(step_idx == 0, front, x)

    d_model = out_ref.shape[1]

    h = step_rows(t, slice(None), slice(None), *h_refs[:n_h // 2])
    ms = jnp.mean(h * h, axis=-1, keepdims=True)
    xn_ref[...] = (h * lax.rsqrt(ms + RMS_EPS) * ng_ref[...]).astype(BF16)

    def half_step(rws_in, proj_in, blk_mix, trig, proj_mix, mixed_mix, rws_out, mixed_out):
        def out_chunk(c0):
            cols = slice(c0, c0 + MM_CHUNK)
            h_res = step_rows(t - 1, rws_out, cols, *h_refs[n_h // 2:])
            out_ref[rws_out, cols] = h_res + _dot(mixed_out[...], wout_ref[:, cols])

        def in_chunk(c0):
            cols = slice(c0, c0 + MM_CHUNK)
            proj_in[:, cols] = _dot(xn_ref[rws_in, :], win_ref[:, cols])

        work = [functools.partial(out_chunk, c0) for c0 in range(0, d_model, MM_CHUNK)]
        work += [functools.partial(in_chunk, c0) for c0 in range(0, IN_COLS, MM_CHUNK)]
        progress = {"ticks": 0, "done": 0}

        def tick():
            progress["ticks"] += 1
            target = min(len(work), -(-len(work) * progress["ticks"] // MIX_TICKS))
            while progress["done"] < target:
                work[progress["done"]]()
                progress["done"] += 1

        _mix_tile(layer, blk_mix, sinks_ref, trig[0], trig[1], lbl_ref, qg_ref, kg_ref, hg_ref,
                  pw_ref, ps_ref, mstack_ref, lmask_ref, proj_mix, mixed_mix, expo_ref, st_ref,
                  kbuf_ref, ksw_ref, vbuf_ref, vsw_ref, uext_ref, tick)
        assert progress["done"] == len(work), "MIX_TICKS must match the tick() calls in _mix_tile"

    rows_a, rows_b = slice(0, TILE), slice(TILE, STEP)
    half_step(rows_a, proja_ref, 2 * t - 2, (cosb_ref, sinb_ref), projb_ref, mixedb_ref, rows_a, mixeda_ref)
    half_step(rows_b, projb_ref, 2 * t - 1, (cosa_ref, sina_ref), proja_ref, mixeda_ref, rows_b, mixedb_ref)


def _layer_call(layer, first, last, h_in, meta, cos, sin, sinks, norm_g, w_in, lb_logits, qg, kg, hg,
                pool_w, pool_scale, w_out, mstack, lmask):
    d = h_in.shape[1]
    n_steps = cos.shape[0] // STEP

    def rows(width, lag, skip_front=False):
        off = lag + (1 if skip_front else 0)
        hi = n_steps - 1 - (1 if skip_front else 0)
        return pl.BlockSpec((STEP, width), lambda t: (jnp.clip(t - off, 0, hi), 0))

    def tile_rows(width, tile_off):
        return pl.BlockSpec((TILE, width), lambda t: (jnp.clip(2 * t + tile_off, 0, 2 * n_steps - 1), 0))

    def whole(shape, single=False):
        idx = lambda t: (0,) * len(shape)
        if single:
            return pl.BlockSpec(shape, idx, pipeline_mode=pl.Buffered(1))
        return pl.BlockSpec(shape, idx)

    in_specs = [pl.BlockSpec(memory_space=pltpu.SMEM)]
    args = [sinks]
    for lag in (0, 1):
        in_specs.append(rows(d, lag, skip_front=first))
        args.append(h_in)
        if first:
            in_specs.append(whole(meta.shape))
            args.append(meta)
    in_specs += [
        tile_rows(LANES, -1), tile_rows(LANES, -1),
        tile_rows(LANES, 0), tile_rows(LANES, 0),
        whole(norm_g.shape),
        pl.BlockSpec(memory_space=pl.ANY),
        whole(lb_logits.shape),
        whole(qg.shape), whole(kg.shape), whole(hg.shape),
        whole(pool_w.shape), whole(pool_scale.shape),
        pl.BlockSpec(memory_space=pl.ANY),
        whole(mstack.shape, single=True), whole(lmask.shape, single=True),
    ]
    args += [cos, sin, cos, sin, norm_g, w_in, lb_logits, qg, kg, hg, pool_w, pool_scale, w_out, mstack, lmask]
    out_rows = (n_steps - 1) * STEP if last else n_steps * STEP
    assert w_in.shape[1] % (2 * TILE) == 0 and w_out.shape[1] % (2 * TILE) == 0
    assert w_out.shape[2] <= IN_COLS
    scratch = [
        pltpu.VMEM(w_in.shape[1:], BF16),
        pltpu.VMEM(w_out.shape[1:], BF16),
        pltpu.SemaphoreType.DMA((2,)),
        pltpu.VMEM((STEP, d), BF16),
        pltpu.VMEM((TILE, IN_COLS), F32),
        pltpu.VMEM((TILE, IN_COLS), F32),
        pltpu.VMEM((TILE, MIX_WIDTH), BF16),
        pltpu.VMEM((TILE, MIX_WIDTH), BF16),
        pltpu.VMEM(((len(LEVELS) + 2) * TILE, A_WIDTH), F32),
        pltpu.VMEM((HA, DK_A, DK_A), F32),
        pltpu.VMEM((KBUF_ROWS, LANES), BF16),
        pltpu.VMEM((KBUF_ROWS, LANES), BF16),
        pltpu.VMEM((KBUF_ROWS, LANES), BF16),
        pltpu.VMEM((KBUF_ROWS, LANES), BF16),
        pltpu.VMEM((max(POOL_WINDOWS) + TILE, C_WIDTH), F32),
    ]
    return pl.pallas_call(
        functools.partial(_layer_kernel, layer, first),
        grid=(n_steps + 1,),
        in_specs=in_specs,
        out_specs=rows(d, 1, skip_front=last),
        out_shape=jax.ShapeDtypeStruct((out_rows, d), F32),
        scratch_shapes=scratch,
        compiler_params=pltpu.CompilerParams(
            dimension_semantics=("arbitrary",), vmem_limit_bytes=VMEM_LIMIT),
        name=f"hybrid_layer{layer}",
    )(*args)


def kernel(x, meta_tokens, lb_logits, norm_g, w_in, q_norm_g, k_norm_g, attn_sinks, hgrn_norm_g,
           pool_w, pool_scale, w_out):
    b, seq, d = x.shape
    depth = w_in.shape[0]
    assert b == 1 and seq % STEP == 0
    assert w_in.shape[2] == IN_COLS and w_out.shape[1] == MIX_WIDTH
    assert meta_tokens.shape[0] == N_META

    p = STEP + seq
    pos = (jnp.arange(p) - (STEP - N_META)).astype(F32)
    half = DH_B // 2
    inv = jnp.power(ROPE_THETA, -jnp.arange(half, dtype=F32) * 2.0 / DH_B)
    ang = pos[:, None] * inv[None, :]
    cos = jnp.tile(jnp.cos(ang), (1, LANES // half))
    sin = jnp.tile(jnp.concatenate([-jnp.sin(ang), jnp.sin(ang)], axis=1), (1, LANES // DH_B))

    mstack_np, lmask_np = _decay_constants()
    mstack = jnp.asarray(mstack_np, BF16)
    lmask = jnp.asarray(lmask_np, F32)

    h = x[0]
    for l in range(depth):
        h = _layer_call(
            l, l == 0, l == depth - 1, h, meta_tokens.astype(F32), cos, sin, attn_sinks[l],
            norm_g[l][None, :], w_in, lb_logits,
            jnp.tile(q_norm_g[l], LANES // DH_B)[None, :], jnp.tile(k_norm_g[l], LANES // DH_B)[None, :],
            hgrn_norm_g[l][None, :], pool_w[l].astype(BF16), pool_scale[l][None, :],
            w_out, mstack, lmask)
    return h[None]
```

```python
import functools

import numpy as np
import jax
import jax.numpy as jnp
from jax import lax
from jax.experimental import pallas as pl
from jax.experimental.pallas import tpu as pltpu

F32 = jnp.float32
BF16 = jnp.bfloat16

N_META = 16
TILE = 128
PAD_FRONT = TILE - N_META
STEP = 2 * TILE
RMS_EPS = 1e-6
NEG_INF = -1e30
LOG_FLOOR = 1e-30
ROPE_THETA = 10000.0

HA, DK_A = 4, 128
A_WIDTH = HA * DK_A
HB, KVH_B, DH_B = 16, 2, 64
B_WIDTH = HB * DH_B
KV_WIDTH = KVH_B * DH_B
POOL_WINDOWS = (2, 4, 8, 16)
CG_C = 128
C_WIDTH = len(POOL_WINDOWS) * CG_C
MIX_WIDTH = A_WIDTH + B_WIDTH + C_WIDTH

OFF_QA = 0
OFF_FA = OFF_QA + A_WIDTH
OFF_IA = OFF_FA + A_WIDTH
OFF_GA = OFF_IA + A_WIDTH
OFF_QB = OFF_GA + A_WIDTH
OFF_KB = OFF_QB + B_WIDTH
OFF_VB = OFF_KB + KV_WIDTH
OFF_GB = OFF_VB + KV_WIDTH
OFF_UC = OFF_GB + B_WIDTH
OFF_GC = OFF_UC + C_WIDTH
IN_COLS = OFF_GC + C_WIDTH
MM_CHUNK = 256

LANES = 128
QB = 64
KWIN = 256
KB_META0 = KWIN - N_META - TILE - QB
KB_PREV = KB_META0 + N_META
KB_CUR = KB_PREV + TILE
KB_META1 = KB_CUR + TILE
KBUF_ROWS = KB_PREV + QB + KWIN
assert KB_META0 >= 16 and KBUF_ROWS - KB_META1 - N_META >= 16
LEVELS = (128, 64, 32, 16, 8, 4, 2)
MIX_TICKS = 1 + HA + 2 + 2 * (TILE // QB) + len(POOL_WINDOWS)
VMEM_LIMIT = 60 * 1024 * 1024


def _decay_constants():
    r = np.arange(TILE)
    mats, masks = [], []
    for blk in LEVELS:
        half = blk // 2
        pos = r % blk
        piv = (r // blk) * blk + half - 1
        m = np.zeros((TILE, TILE), np.float32)
        for i in range(TILE):
            if pos[i] >= half:
                m[i, piv[i] + 1:i + 1] = 1.0
            else:
                m[i, i + 1:piv[i] + 1] = 1.0
        mats.append(m)
        same = (r[:, None] // blk) == (r[None, :] // blk)
        masks.append((same & (pos[:, None] >= half) & (pos[None, :] < half)).astype(np.float32))
    mats.append(np.tril(np.ones((TILE, TILE), np.float32)))
    masks.append(np.eye(TILE, dtype=np.float32))
    stack = np.concatenate(mats, axis=0)
    return np.concatenate([stack, stack], axis=1), np.stack(masks, axis=0)


def _dot(a, b):
    return jnp.dot(a, b, preferred_element_type=F32)


def _dot_nt(a, b):
    return lax.dot_general(a, b, (((1,), (1,)), ((), ())), preferred_element_type=F32)


def _dot_tn(a, b):
    return lax.dot_general(a, b, (((0,), (0,)), ((), ())), preferred_element_type=F32)


def _sigmoid(x):
    return 0.5 * jnp.tanh(0.5 * x) + 0.5


def _silu(x):
    t = 0.5 * x
    return t * jnp.tanh(t) + t


def _mix_tile(layer, blk, sinks_ref, cos_ref, sin_ref, lbl_ref, qg_ref, kg_ref, hg_ref,
              pw_ref, ps_ref, mstack_ref, lmask_ref, proj_ref, mixed_ref, expo_ref, st_ref,
              kbuf_ref, ksw_ref, vbuf_ref, vsw_ref, uext_ref, tick):
    rloc = lax.broadcasted_iota(jnp.int32, (TILE, 1), 0)
    row = blk * TILE + rloc
    valid = (row >= PAD_FRONT).astype(F32)

    rows = [lbl_ref[j:j + 1, :] for j in range(lbl_ref.shape[0])]
    mx = functools.reduce(jnp.maximum, rows)
    es = [jnp.exp(x - mx) for x in rows]
    lb = sum(es[1:layer + 1], jnp.zeros_like(mx)) / sum(es)

    z = proj_ref[:, OFF_FA:OFF_FA + A_WIDTH]
    sig = _sigmoid(z)
    f = lb + (1.0 - lb) * sig
    logf = jnp.log(jnp.maximum(f, LOG_FLOOR)) * valid
    kk = (1.0 - lb) * (1.0 - sig) * valid
    proj_ref[:, OFF_QA:OFF_QA + A_WIDTH] = _silu(proj_ref[:, OFF_QA:OFF_QA + A_WIDTH])
    proj_ref[:, OFF_FA:OFF_FA + A_WIDTH] = kk
    lf_hi = logf.astype(BF16)
    lf_lo = (logf - lf_hi.astype(F32)).astype(BF16)
    expo_ref[...] = _dot(mstack_ref[...], jnp.concatenate([lf_hi, lf_lo], axis=0))
    n_lv = len(LEVELS)
    tick()

    for hd in range(HA):
        cs = slice(hd * DK_A, (hd + 1) * DK_A)
        qf_h = proj_ref[:, OFF_QA + hd * DK_A:OFF_QA + (hd + 1) * DK_A]
        kk_h = proj_ref[:, OFF_FA + hd * DK_A:OFF_FA + (hd + 1) * DK_A]
        v_h = proj_ref[:, OFF_IA + hd * DK_A:OFF_IA + (hd + 1) * DK_A].astype(BF16)
        attn = _dot_nt(qf_h.astype(BF16), kk_h.astype(BF16)) * lmask_ref[n_lv]
        for li, bsz in enumerate(LEVELS):
            upper = (rloc & (bsz - 1)) >= (bsz // 2)
            xb = (jnp.where(upper, qf_h, kk_h) * jnp.exp(expo_ref[li * TILE:(li + 1) * TILE, cs])).astype(BF16)
            attn = attn + _dot_nt(xb, xb) * lmask_ref[li]
        o = _dot(attn.astype(BF16), v_h)
        st = st_ref[hd]
        g_inc = expo_ref[n_lv * TILE:(n_lv + 1) * TILE, cs]
        q_dec = (qf_h * jnp.exp(g_inc)).astype(BF16)
        o = o + _dot_nt(q_dec, st.astype(BF16))
        g_tot = g_inc[TILE - 1:TILE]
        k_dec = (kk_h * jnp.exp(g_tot - g_inc)).astype(BF16)
        a_tot = jnp.exp(g_tot)
        st_ref[hd] = a_tot * st + _dot_tn(v_h, k_dec)
        ya = o * lax.rsqrt(jnp.mean(o * o, axis=-1, keepdims=True) + RMS_EPS) * hg_ref[...]
        ga = proj_ref[:, OFF_GA + hd * DK_A:OFF_GA + (hd + 1) * DK_A]
        mixed_ref[:, hd * DK_A:(hd + 1) * DK_A] = (ya * _silu(ga)).astype(BF16)
        tick()

    lane = lax.broadcasted_iota(jnp.int32, (1, LANES), 1)
    lo_half = lane < DH_B
    first = (lane & (DH_B - 1)) < DH_B // 2
    cos = cos_ref[...]
    sin = sin_ref[...]

    def norm_rope(x, g):
        sq = x * x
        s0 = jnp.sum(jnp.where(lo_half, sq, 0.0), axis=-1, keepdims=True)
        s1 = jnp.sum(jnp.where(lo_half, 0.0, sq), axis=-1, keepdims=True)
        msq = jnp.where(lo_half, s0, s1) * (1.0 / DH_B)
        y = x * lax.rsqrt(msq + RMS_EPS) * g
        rot = jnp.where(first, pltpu.roll(y, LANES - DH_B // 2, axis=1),
                        pltpu.roll(y, DH_B // 2, axis=1))
        return y * cos + rot * sin

    kc = norm_rope(proj_ref[:, OFF_KB:OFF_KB + KV_WIDTH], kg_ref[...])
    vc = proj_ref[:, OFF_VB:OFF_VB + KV_WIDTH]
    is_meta_tile = blk == 0
    for buf, val in ((kbuf_ref, kc), (ksw_ref, pltpu.roll(kc, DH_B, axis=1)),
                     (vbuf_ref, vc), (vsw_ref, pltpu.roll(vc, DH_B, axis=1))):
        val = val.astype(BF16)
        buf[KB_CUR:KB_CUR + TILE, :] = val
        for m0 in (KB_META0, KB_META1):
            buf[m0:m0 + N_META, :] = jnp.where(is_meta_tile, val[PAD_FRONT:], buf[m0:m0 + N_META, :])
    tick()

    scale = DH_B ** -0.5
    qcols = []
    for j in range(HB // 2):
        qj = norm_rope(proj_ref[:, OFF_QB + j * LANES:OFF_QB + (j + 1) * LANES], qg_ref[...]) * scale
        qcols.append((jnp.where(lo_half, qj, 0.0).astype(BF16), jnp.where(lo_half, 0.0, qj).astype(BF16)))
    tick()

    grp = HB // KVH_B
    heads_same = [hh for hh in range(HB) if (hh % 2) == (hh // grp)]
    heads_swap = [hh for hh in range(HB) if (hh % 2) != (hh // grp)]
    col = lax.broadcasted_iota(jnp.int32, (1, KWIN), 1)
    big = jnp.int32(1 << 30)
    for sb, (w0, sink_col) in enumerate(((0, 0), (KBUF_ROWS - KWIN, KWIN - 1))):
        q0 = sb * QB
        brow = w0 + col
        in_meta0 = (brow >= KB_META0) & (brow < KB_PREV)
        in_meta1 = (brow >= KB_META1) & (brow < KB_META1 + N_META)
        in_band = (brow >= KB_PREV) & (brow < KB_META1)
        kj = jnp.where(in_meta0, PAD_FRONT + brow - KB_META0,
                       jnp.where(in_meta1, PAD_FRONT + brow - KB_META1,
                                 jnp.where(in_band, (blk - 1) * TILE + brow - KB_PREV, big)))
        qi = blk * TILE + q0 + lax.broadcasted_iota(jnp.int32, (QB, 1), 0)
        allowed = (kj <= qi) & (in_meta0 | in_meta1 | ((kj > qi - TILE) & (kj >= TILE)))

        outs = {}
        for heads, k_ref, v_ref in ((heads_same, kbuf_ref, vbuf_ref), (heads_swap, ksw_ref, vsw_ref)):
            lhs = jnp.concatenate([qcols[hh // 2][hh % 2][q0:q0 + QB] for hh in heads], axis=0)
            s_all = _dot_nt(lhs, k_ref[w0:w0 + KWIN, :])
            ps, inv_den = [], []
            for n, hh in enumerate(heads):
                fill = jnp.where(col == sink_col, sinks_ref[hh], NEG_INF)
                s = jnp.where(allowed, s_all[n * QB:(n + 1) * QB], fill)
                pr = jnp.exp(s - jnp.max(s, axis=-1, keepdims=True))
                inv_den.append(1.0 / jnp.sum(pr, axis=-1, keepdims=True))
                ps.append(pr.astype(BF16))
            o_all = _dot(jnp.concatenate(ps, axis=0), v_ref[w0:w0 + KWIN, :])
            for n, hh in enumerate(heads):
                outs[hh] = o_all[n * QB:(n + 1) * QB] * inv_den[n]
            tick()
        for j in range(HB // 2):
            yb = jnp.where(lo_half, outs[2 * j], outs[2 * j + 1])
            gb = proj_ref[q0:q0 + QB, OFF_GB + j * LANES:OFF_GB + (j + 1) * LANES]
            mixed_ref[q0:q0 + QB, A_WIDTH + j * LANES:A_WIDTH + (j + 1) * LANES] = (
                yb * _silu(gb)).astype(BF16)

    for buf in (kbuf_ref, ksw_ref, vbuf_ref, vsw_ref):
        buf[KB_PREV:KB_CUR, :] = buf[KB_CUR:KB_CUR + TILE, :]

    wmax = max(POOL_WINDOWS)
    ug = proj_ref[:, OFF_UC:OFF_UC + C_WIDTH] * valid
    uext_ref[wmax:wmax + TILE, :] = ug
    n_valid = jnp.maximum(row - (PAD_FRONT - 1), 0)
    for g, w in enumerate(POOL_WINDOWS):
        cs = slice(g * CG_C, (g + 1) * CG_C)
        acc = uext_ref[wmax:wmax + TILE, cs]
        for s in range(1, w):
            acc = acc + uext_ref[wmax - s:wmax - s + TILE, cs]
        cnt = (n_valid - jnp.maximum(row - w - (PAD_FRONT - 1), 0)).astype(F32)
        pooled = (acc / jnp.maximum(cnt, 1.0) - ug[:, cs]) * valid
        y = _dot(pooled.astype(BF16), pw_ref[g]) * ps_ref[:, cs]
        gc = proj_ref[:, OFF_GC + g * CG_C:OFF_GC + (g + 1) * CG_C]
        mixed_ref[:, A_WIDTH + B_WIDTH + g * CG_C:A_WIDTH + B_WIDTH + (g + 1) * CG_C] = (
            y * _silu(gc)).astype(BF16)
        tick()
    uext_ref[0:wmax, :] = ug[TILE - wmax:]


def _layer_kernel(layer, first, sinks_ref, *refs):
    n_h = 4 if first else 2
    h_refs, refs = refs[:n_h], refs[n_h:]
    (cosb_ref, sinb_ref, cosa_ref, sina_ref, ng_ref, win_hbm, lbl_ref, qg_ref, kg_ref, hg_ref, pw_ref,
     ps_ref, wout_hbm, mstack_ref, lmask_ref, out_ref, win_ref, wout_ref, wsem, xn_ref, proja_ref,
     projb_ref, mixeda_ref, mixedb_ref, expo_ref, st_ref, kbuf_ref, ksw_ref, vbuf_ref, vsw_ref,
     uext_ref) = refs
    t = pl.program_id(0)

    @pl.when(t == 0)
    def _load_weights_and_zero_state():
        stage = (proja_ref, projb_ref)
        for w_hbm, w_vmem in ((win_hbm, win_ref), (wout_hbm, wout_ref)):
            width = w_vmem.shape[1]
            n_chunks = w_vmem.shape[0] // TILE

            def chunk_copy(k, slot, w_hbm=w_hbm, width=width):
                return pltpu.make_async_copy(w_hbm.at[layer, pl.ds(k * TILE, TILE), :],
                                             stage[slot].at[:, 0:width], wsem.at[slot])

            chunk_copy(0, 0).start()
            chunk_copy(1, 1).start()

            def pair(i, carry, w_vmem=w_vmem, width=width, n_chunks=n_chunks, chunk_copy=chunk_copy):
                for slot in (0, 1):
                    k = 2 * i + slot
                    chunk_copy(k, slot).wait()
                    w_vmem[pl.ds(pl.multiple_of(k * TILE, TILE), TILE), :] = stage[slot][:, 0:width].astype(BF16)

                    @pl.when(k + 2 < n_chunks)
                    def _next():
                        chunk_copy(k + 2, slot).start()
                return carry

            lax.fori_loop(0, n_chunks // 2, pair, 0)

        for ref in (proja_ref, projb_ref, mixeda_ref, mixedb_ref, st_ref, kbuf_ref, ksw_ref,
                    vbuf_ref, vsw_ref, uext_ref):
            ref[...] = jnp.zeros_like(ref)

    def step_rows(step_idx, rws, cols, x_ref, meta_ref=None):
        x = x_ref[rws, cols]
        if not first:
            return x
        if rws.stop is not None and rws.stop < STEP:
            return jnp.where(step_idx == 0, 0.0, x)
        n_rows = x.shape[0]
        front = jnp.concatenate([jnp.zeros((n_rows - N_META, x.shape[1]), F32), meta_ref[:, cols]], axis=0)
        return jnp.where(step_idx == 0, front, x)

    d_model = out_ref.shape[1]

    h = step_rows(t, slice(None), slice(None), *h_refs[:n_h // 2])
    ms = jnp.mean(h * h, axis=-1, keepdims=True)
    xn_ref[...] = (h * lax.rsqrt(ms + RMS_EPS) * ng_ref[...]).astype(BF16)

    def half_step(rws_in, proj_in, blk_mix, trig, proj_mix, mixed_mix, rws_out, mixed_out):
        def out_chunk(c0):
            cols = slice(c0, c0 + MM_CHUNK)
            h_res = step_rows(t - 1, rws_out, cols, *h_refs[n_h // 2:])
            out_ref[rws_out, cols] = h_res + _dot(mixed_out[...], wout_ref[:, cols])

        def in_chunk(c0):
            cols = slice(c0, c0 + MM_CHUNK)
            proj_in[:, cols] = _dot(xn_ref[rws_in, :], win_ref[:, cols])

        work = [functools.partial(out_chunk, c0) for c0 in range(0, d_model, MM_CHUNK)]
        work += [functools.partial(in_chunk, c0) for c0 in range(0, IN_COLS, MM_CHUNK)]
        progress = {"ticks": 0, "done": 0}

        def tick():
            progress["ticks"] += 1
            target = min(len(work), -(-len(work) * progress["ticks"] // MIX_TICKS))
            while progress["done"] < target:
                work[progress["done"]]()
                progress["done"] += 1

        _mix_tile(layer, blk_mix, sinks_ref, trig[0], trig[1], lbl_ref, qg_ref, kg_ref, hg_ref,
                  pw_ref, ps_ref, mstack_ref, lmask_ref, proj_mix, mixed_mix, expo_ref, st_ref,
                  kbuf_ref, ksw_ref, vbuf_ref, vsw_ref, uext_ref, tick)
        assert progress["done"] == len(work), "MIX_TICKS must match the tick() calls in _mix_tile"

    rows_a, rows_b = slice(0, TILE), slice(TILE, STEP)
    half_step(rows_a, proja_ref, 2 * t - 2, (cosb_ref, sinb_ref), projb_ref, mixedb_ref, rows_a, mixeda_ref)
    half_step(rows_b, projb_ref, 2 * t - 1, (cosa_ref, sina_ref), proja_ref, mixeda_ref, rows_b, mixedb_ref)


def _layer_call(layer, first, last, h_in, meta, cos, sin, sinks, norm_g, w_in, lb_logits, qg, kg, hg,
                pool_w, pool_scale, w_out, mstack, lmask):
    d = h_in.shape[1]
    n_steps = cos.shape[0] // STEP

    def rows(width, lag, skip_front=False):
        off = lag + (1 if skip_front else 0)
        hi = n_steps - 1 - (1 if skip_front else 0)
        return pl.BlockSpec((STEP, width), lambda t: (jnp.clip(t - off, 0, hi), 0))

    def tile_rows(width, tile_off):
        return pl.BlockSpec((TILE, width), lambda t: (jnp.clip(2 * t + tile_off, 0, 2 * n_steps - 1), 0))

    def whole(shape, single=False):
        idx = lambda t: (0,) * len(shape)
        if single:
            return pl.BlockSpec(shape, idx, pipeline_mode=pl.Buffered(1))
        return pl.BlockSpec(shape, idx)

    in_specs = [pl.BlockSpec(memory_space=pltpu.SMEM)]
    args = [sinks]
    for lag in (0, 1):
        in_specs.append(rows(d, lag, skip_front=first))
        args.append(h_in)
        if first:
            in_specs.append(whole(meta.shape))
            args.append(meta)
    in_specs += [
        tile_rows(LANES, -1), tile_rows(LANES, -1),
        tile_rows(LANES, 0), tile_rows(LANES, 0),
        whole(norm_g.shape),
        pl.BlockSpec(memory_space=pl.ANY),
        whole(lb_logits.shape),
        whole(qg.shape), whole(kg.shape), whole(hg.shape),
        whole(pool_w.shape), whole(pool_scale.shape),
        pl.BlockSpec(memory_space=pl.ANY),
        whole(mstack.shape, single=True), whole(lmask.shape, single=True),
    ]
    args += [cos, sin, cos, sin, norm_g, w_in, lb_logits, qg, kg, hg, pool_w, pool_scale, w_out, mstack, lmask]
    out_rows = (n_steps - 1) * STEP if last else n_steps * STEP
    assert w_in.shape[1] % (2 * TILE) == 0 and w_out.shape[1] % (2 * TILE) == 0
    assert w_out.shape[2] <= IN_COLS
    scratch = [
        pltpu.VMEM(w_in.shape[1:], BF16),
        pltpu.VMEM(w_out.shape[1:], BF16),
        pltpu.SemaphoreType.DMA((2,)),
        pltpu.VMEM((STEP, d), BF16),
        pltpu.VMEM((TILE, IN_COLS), F32),
        pltpu.VMEM((TILE, IN_COLS), F32),
        pltpu.VMEM((TILE, MIX_WIDTH), BF16),
        pltpu.VMEM((TILE, MIX_WIDTH), BF16),
        pltpu.VMEM(((len(LEVELS) + 1) * TILE, A_WIDTH), F32),
        pltpu.VMEM((HA, DK_A, DK_A), F32),
        pltpu.VMEM((KBUF_ROWS, LANES), BF16),
        pltpu.VMEM((KBUF_ROWS, LANES), BF16),
        pltpu.VMEM((KBUF_ROWS, LANES), BF16),
        pltpu.VMEM((KBUF_ROWS, LANES), BF16),
        pltpu.VMEM((max(POOL_WINDOWS) + TILE, C_WIDTH), F32),
    ]
    return pl.pallas_call(
        functools.partial(_layer_kernel, layer, first),
        grid=(n_steps + 1,),
        in_specs=in_specs,
        out_specs=rows(d, 1, skip_front=last),
        out_shape=jax.ShapeDtypeStruct((out_rows, d), F32),
        scratch_shapes=scratch,
        compiler_params=pltpu.CompilerParams(
            dimension_semantics=("arbitrary",), vmem_limit_bytes=VMEM_LIMIT),
        name=f"hybrid_layer{layer}",
    )(*args)


def kernel(x, meta_tokens, lb_logits, norm_g, w_in, q_norm_g, k_norm_g, attn_sinks, hgrn_norm_g,
           pool_w, pool_scale, w_out):
    b, seq, d = x.shape
    depth = w_in.shape[0]
    assert b == 1 and seq % STEP == 0
    assert w_in.shape[2] == IN_COLS and w_out.shape[1] == MIX_WIDTH
    assert meta_tokens.shape[0] == N_META

    p = STEP + seq
    pos = (jnp.arange(p) - (STEP - N_META)).astype(F32)
    half = DH_B // 2
    inv = jnp.power(ROPE_THETA, -jnp.arange(half, dtype=F32) * 2.0 / DH_B)
    ang = pos[:, None] * inv[None, :]
    cos = jnp.tile(jnp.cos(ang), (1, LANES // half))
    sin = jnp.tile(jnp.concatenate([-jnp.sin(ang), jnp.sin(ang)], axis=1), (1, LANES // DH_B))

    mstack_np, lmask_np = _decay_constants()
    mstack = jnp.asarray(mstack_np, BF16)
    lmask = jnp.asarray(lmask_np, F32)

    h = x[0]
    for l in range(depth):
        h = _layer_call(
            l, l == 0, l == depth - 1, h, meta_tokens.astype(F32), cos, sin, attn_sinks[l],
            norm_g[l][None, :], w_in, lb_logits,
            jnp.tile(q_norm_g[l], LANES // DH_B)[None, :], jnp.tile(k_norm_g[l], LANES // DH_B)[None, :],
            hgrn_norm_g[l][None, :], pool_w[l].astype(BF16), pool_scale[l][None, :],
            w_out, mstack, lmask)
    return h[None]
```

```python
import functools

import numpy as np
import jax
import jax.numpy as jnp
from jax import lax
from jax.experimental import pallas as pl
from jax.experimental.pallas import tpu as pltpu

F32 = jnp.float32
BF16 = jnp.bfloat16

N_META = 16
TILE = 128
PAD_FRONT = TILE - N_META
STEP = 2 * TILE
RMS_EPS = 1e-6
NEG_INF = -1e30
LOG_FLOOR = 1e-30
ROPE_THETA = 10000.0

HA, DK_A = 4, 128
A_WIDTH = HA * DK_A
HB, KVH_B, DH_B = 16, 2, 64
B_WIDTH = HB * DH_B
KV_WIDTH = KVH_B * DH_B
POOL_WINDOWS = (2, 4, 8, 16)
CG_C = 128
C_WIDTH = len(POOL_WINDOWS) * CG_C
MIX_WIDTH = A_WIDTH + B_WIDTH + C_WIDTH

OFF_QA = 0
OFF_FA = OFF_QA + A_WIDTH
OFF_IA = OFF_FA + A_WIDTH
OFF_GA = OFF_IA + A_WIDTH
OFF_QB = OFF_GA + A_WIDTH
OFF_KB = OFF_QB + B_WIDTH
OFF_VB = OFF_KB + KV_WIDTH
OFF_GB = OFF_VB + KV_WIDTH
OFF_UC = OFF_GB + B_WIDTH
OFF_GC = OFF_UC + C_WIDTH
IN_COLS = OFF_GC + C_WIDTH
MM_CHUNK = 256

LANES = 128
QB = 64
KWIN = 256
KB_META0 = KWIN - N_META - TILE - QB
KB_PREV = KB_META0 + N_META
KB_CUR = KB_PREV + TILE
KB_META1 = KB_CUR + TILE
KBUF_ROWS = KB_PREV + QB + KWIN
assert KB_META0 >= 16 and KBUF_ROWS - KB_META1 - N_META >= 16
LEVELS = (128, 64, 32, 16, 8, 4, 2)
MIX_TICKS = 1 + HA + 2 + 2 * (TILE // QB) + len(POOL_WINDOWS)
VMEM_LIMIT = 60 * 1024 * 1024


def _decay_constants():
    r = np.arange(TILE)
    mats, masks = [], []
    for blk in LEVELS:
        half = blk // 2
        pos = r % blk
        piv = (r // blk) * blk + half - 1
        m = np.zeros((TILE, TILE), np.float32)
        for i in range(TILE):
            if pos[i] >= half:
                m[i, piv[i] + 1:i + 1] = 1.0
            else:
                m[i, i + 1:piv[i] + 1] = 1.0
        mats.append(m)
        same = (r[:, None] // blk) == (r[None, :] // blk)
        masks.append((same & (pos[:, None] >= half) & (pos[None, :] < half)).astype(np.float32))
    mats.append(np.tril(np.ones((TILE, TILE), np.float32)))
    masks.append(np.eye(TILE, dtype=np.float32))
    stack = np.concatenate(mats, axis=0)
    return np.concatenate([stack, stack], axis=1), np.stack(masks, axis=0)


def _dot(a, b):
    return jnp.dot(a, b, preferred_element_type=F32)


def _dot_nt(a, b):
    return lax.dot_general(a, b, (((1,), (1,)), ((), ())), preferred_element_type=F32)


def _dot_tn(a, b):
    return lax.dot_general(a, b, (((0,), (0,)), ((), ())), preferred_element_type=F32)


def _sigmoid(x):
    return 0.5 * jnp.tanh(0.5 * x) + 0.5


def _silu(x):
    t = 0.5 * x
    return t * jnp.tanh(t) + t


def _mix_tile(layer, blk, sinks_ref, cos_ref, sin_ref, lbl_ref, qg_ref, kg_ref, hg_ref,
              pw_ref, ps_ref, mstack_ref, lmask_ref, proj_ref, mixed_ref, expo_ref, st_ref,
              kbuf_ref, ksw_ref, vbuf_ref, vsw_ref, uext_ref, tick):
    rloc = lax.broadcasted_iota(jnp.int32, (TILE, 1), 0)
    row = blk * TILE + rloc
    valid = (row >= PAD_FRONT).astype(F32)

    rows = [lbl_ref[j:j + 1, :] for j in range(lbl_ref.shape[0])]
    mx = functools.reduce(jnp.maximum, rows)
    es = [jnp.exp(x - mx) for x in rows]
    lb = sum(es[1:layer + 1], jnp.zeros_like(mx)) / sum(es)

    z = proj_ref[:, OFF_FA:OFF_FA + A_WIDTH]
    sig = _sigmoid(z)
    f = lb + (1.0 - lb) * sig
    logf = jnp.log(jnp.maximum(f, LOG_FLOOR)) * valid
    kk = (1.0 - lb) * (1.0 - sig) * valid
    proj_ref[:, OFF_QA:OFF_QA + A_WIDTH] = _silu(proj_ref[:, OFF_QA:OFF_QA + A_WIDTH])
    proj_ref[:, OFF_FA:OFF_FA + A_WIDTH] = kk
    lf_hi = logf.astype(BF16)
    lf_lo = (logf - lf_hi.astype(F32)).astype(BF16)
    expo_ref[...] = _dot(mstack_ref[...], jnp.concatenate([lf_hi, lf_lo], axis=0))
    n_lv = len(LEVELS)
    tick()

    for hd in range(HA):
        cs = slice(hd * DK_A, (hd + 1) * DK_A)
        qf_h = proj_ref[:, OFF_QA + hd * DK_A:OFF_QA + (hd + 1) * DK_A]
        kk_h = proj_ref[:, OFF_FA + hd * DK_A:OFF_FA + (hd + 1) * DK_A]
        v_h = proj_ref[:, OFF_IA + hd * DK_A:OFF_IA + (hd + 1) * DK_A].astype(BF16)
        attn = _dot_nt(qf_h.astype(BF16), kk_h.astype(BF16)) * lmask_ref[n_lv]
        for li, bsz in enumerate(LEVELS):
            upper = (rloc & (bsz - 1)) >= (bsz // 2)
            xb = (jnp.where(upper, qf_h, kk_h) * jnp.exp(expo_ref[li * TILE:(li + 1) * TILE, cs])).astype(BF16)
            attn = attn + _dot_nt(xb, xb) * lmask_ref[li]
        o = _dot(attn.astype(BF16), v_h)
        st = st_ref[hd]
        g_inc = expo_ref[n_lv * TILE:(n_lv + 1) * TILE, cs]
        q_dec = (qf_h * jnp.exp(g_inc)).astype(BF16)
        o = o + _dot_nt(q_dec, st.astype(BF16))
        g_tot = g_inc[TILE - 1:TILE]
        k_dec = (kk_h * jnp.exp(g_tot - g_inc)).astype(BF16)
        a_tot = jnp.exp(g_tot)
        st_ref[hd] = a_tot * st + _dot_tn(v_h, k_dec)
        ya = o * lax.rsqrt(jnp.mean(o * o, axis=-1, keepdims=True) + RMS_EPS) * hg_ref[...]
        ga = proj_ref[:, OFF_GA + hd * DK_A:OFF_GA + (hd + 1) * DK_A]
        mixed_ref[:, hd * DK_A:(hd + 1) * DK_A] = (ya * _silu(ga)).astype(BF16)
        tick()

    lane = lax.broadcasted_iota(jnp.int32, (1, LANES), 1)
    lo_half = lane < DH_B
    first = (lane & (DH_B - 1)) < DH_B // 2
    cos = cos_ref[...]
    sin = sin_ref[...]

    def norm_rope(x, g):
        sq = x * x
        s0 = jnp.sum(jnp.where(lo_half, sq, 0.0), axis=-1, keepdims=True)
        s1 = jnp.sum(jnp.where(lo_half, 0.0, sq), axis=-1, keepdims=True)
        msq = jnp.where(lo_half, s0, s1) * (1.0 / DH_B)
        y = x * lax.rsqrt(msq + RMS_EPS) * g
        rot = jnp.where(first, pltpu.roll(y, LANES - DH_B // 2, axis=1),
                        pltpu.roll(y, DH_B // 2, axis=1))
        return y * cos + rot * sin

    kc = norm_rope(proj_ref[:, OFF_KB:OFF_KB + KV_WIDTH], kg_ref[...])
    vc = proj_ref[:, OFF_VB:OFF_VB + KV_WIDTH]
    is_meta_tile = blk == 0
    for buf, val in ((kbuf_ref, kc), (ksw_ref, pltpu.roll(kc, DH_B, axis=1)),
                     (vbuf_ref, vc), (vsw_ref, pltpu.roll(vc, DH_B, axis=1))):
        val = val.astype(BF16)
        buf[KB_CUR:KB_CUR + TILE, :] = val
        for m0 in (KB_META0, KB_META1):
            buf[m0:m0 + N_META, :] = jnp.where(is_meta_tile, val[PAD_FRONT:], buf[m0:m0 + N_META, :])
    tick()

    scale = DH_B ** -0.5
    qcols = []
    for j in range(HB // 2):
        qj = norm_rope(proj_ref[:, OFF_QB + j * LANES:OFF_QB + (j + 1) * LANES], qg_ref[...]) * scale
        qcols.append((jnp.where(lo_half, qj, 0.0).astype(BF16), jnp.where(lo_half, 0.0, qj).astype(BF16)))
    tick()

    grp = HB // KVH_B
    heads_same = [hh for hh in range(HB) if (hh % 2) == (hh // grp)]
    heads_swap = [hh for hh in range(HB) if (hh % 2) != (hh // grp)]
    col = lax.broadcasted_iota(jnp.int32, (1, KWIN), 1)
    big = jnp.int32(1 << 30)
    for sb, (w0, sink_col) in enumerate(((0, 0), (KBUF_ROWS - KWIN, KWIN - 1))):
        q0 = sb * QB
        brow = w0 + col
        in_meta0 = (brow >= KB_META0) & (brow < KB_PREV)
        in_meta1 = (brow >= KB_META1) & (brow < KB_META1 + N_META)
        in_band = (brow >= KB_PREV) & (brow < KB_META1)
        kj = jnp.where(in_meta0, PAD_FRONT + brow - KB_META0,
                       jnp.where(in_meta1, PAD_FRONT + brow - KB_META1,
                                 jnp.where(in_band, (blk - 1) * TILE + brow - KB_PREV, big)))
        qi = blk * TILE + q0 + lax.broadcasted_iota(jnp.int32, (QB, 1), 0)
        allowed = (kj <= qi) & (in_meta0 | in_meta1 | ((kj > qi - TILE) & (kj >= TILE)))

        outs = {}
        for heads, k_ref, v_ref in ((heads_same, kbuf_ref, vbuf_ref), (heads_swap, ksw_ref, vsw_ref)):
            lhs = jnp.concatenate([qcols[hh // 2][hh % 2][q0:q0 + QB] for hh in heads], axis=0)
            s_all = _dot_nt(lhs, k_ref[w0:w0 + KWIN, :])
            ps, inv_den = [], []
            for n, hh in enumerate(heads):
                fill = jnp.where(col == sink_col, sinks_ref[hh], NEG_INF)
                s = jnp.where(allowed, s_all[n * QB:(n + 1) * QB], fill)
                pr = jnp.exp(s - jnp.max(s, axis=-1, keepdims=True))
                inv_den.append(1.0 / jnp.sum(pr, axis=-1, keepdims=True))
                ps.append(pr.astype(BF16))
            o_all = _dot(jnp.concatenate(ps, axis=0), v_ref[w0:w0 + KWIN, :])
            for n, hh in enumerate(heads):
                outs[hh] = o_all[n * QB:(n + 1) * QB] * inv_den[n]
            tick()
        for j in range(HB // 2):
            yb = jnp.where(lo_half, outs[2 * j], outs[2 * j + 1])
            gb = proj_ref[q0:q0 + QB, OFF_GB + j * LANES:OFF_GB + (j + 1) * LANES]
            mixed_ref[q0:q0 + QB, A_WIDTH + j * LANES:A_WIDTH + (j + 1) * LANES] = (
                yb * _silu(gb)).astype(BF16)

    for buf in (kbuf_ref, ksw_ref, vbuf_ref, vsw_ref):
        buf[KB_PREV:KB_CUR, :] = buf[KB_CUR:KB_CUR + TILE, :]

    wmax = max(POOL_WINDOWS)
    ug = proj_ref[:, OFF_UC:OFF_UC + C_WIDTH] * valid
    uext_ref[wmax:wmax + TILE, :] = ug
    n_valid = jnp.maximum(row - (PAD_FRONT - 1), 0)
    for g, w in enumerate(POOL_WINDOWS):
        cs = slice(g * CG_C, (g + 1) * CG_C)
        acc = uext_ref[wmax:wmax + TILE, cs]
        for s in range(1, w):
            acc = acc + uext_ref[wmax - s:wmax - s + TILE, cs]
        cnt = (n_valid - jnp.maximum(row - w - (PAD_FRONT - 1), 0)).astype(F32)
        pooled = (acc / jnp.maximum(cnt, 1.0) - ug[:, cs]) * valid
        y = _dot(pooled.astype(BF16), pw_ref[g]) * ps_ref[:, cs]
        gc = proj_ref[:, OFF_GC + g * CG_C:OFF_GC + (g + 1) * CG_C]
        mixed_ref[:, A_WIDTH + B_WIDTH + g * CG_C:A_WIDTH + B_WIDTH + (g + 1) * CG_C] = (
            y * _silu(gc)).astype(BF16)
        tick()
    uext_ref[0:wmax, :] = ug[TILE - wmax:]


def _layer_kernel(layer, first, sinks_ref, *refs):
    n_h = 4 if first else 2
    h_refs, refs = refs[:n_h], refs[n_h:]
    (cosb_ref, sinb_ref, cosa_ref, sina_ref, ng_ref, win_hbm, lbl_ref, qg_ref, kg_ref, hg_ref, pw_ref,
     ps_ref, wout_hbm, mstack_ref, lmask_ref, out_ref, win_ref, wout_ref, wsem, xn_ref, proja_ref,
     projb_ref, mixeda_ref, mixedb_ref, expo_ref, st_ref, kbuf_ref, ksw_ref, vbuf_ref, vsw_ref,
     uext_ref) = refs
    t = pl.program_id(0)

    @pl.when(t == 0)
    def _load_weights_and_zero_state():
        stage = (proja_ref, projb_ref)
        for w_hbm, w_vmem in ((win_hbm, win_ref), (wout_hbm, wout_ref)):
            width = w_vmem.shape[1]
            n_chunks = w_vmem.shape[0] // TILE

            def chunk_copy(k, slot, w_hbm=w_hbm, width=width):
                return pltpu.make_async_copy(w_hbm.at[layer, pl.ds(k * TILE, TILE), :],
                                             stage[slot].at[:, 0:width], wsem.at[slot])

            chunk_copy(0, 0).start()
            chunk_copy(1, 1).start()

            def pair(i, carry, w_vmem=w_vmem, width=width, n_chunks=n_chunks, chunk_copy=chunk_copy):
                for slot in (0, 1):
                    k = 2 * i + slot
                    chunk_copy(k, slot).wait()
                    w_vmem[pl.ds(pl.multiple_of(k * TILE, TILE), TILE), :] = stage[slot][:, 0:width].astype(BF16)

                    @pl.when(k + 2 < n_chunks)
                    def _next():
                        chunk_copy(k + 2, slot).start()
                return carry

            lax.fori_loop(0, n_chunks // 2, pair, 0)

        for ref in (proja_ref, projb_ref, mixeda_ref, mixedb_ref, st_ref, kbuf_ref, ksw_ref,
                    vbuf_ref, vsw_ref, uext_ref):
            ref[...] = jnp.zeros_like(ref)

    def step_rows(step_idx, rws, cols, x_ref, meta_ref=None):
        x = x_ref[rws, cols]
        if not first:
            return x
        if rws.stop is not None and rws.stop < STEP:
            return jnp.where(step_idx == 0, 0.0, x)
        n_rows = x.shape[0]
        front = jnp.concatenate([jnp.zeros((n_rows - N_META, x.shape[1]), F32), meta_ref[:, cols]], axis=0)
        return jnp.where(step_idx == 0, front, x)

    d_model = out_ref.shape[1]
    last_step = pl.num_programs(0) - 1
    rows_a, rows_b = slice(0, TILE), slice(TILE, STEP)

    def norm_rows():
        h = step_rows(t, slice(None), slice(None), *h_refs[:n_h // 2])
        ms = jnp.mean(h * h, axis=-1, keepdims=True)
        xn_ref[...] = (h * lax.rsqrt(ms + RMS_EPS) * ng_ref[...]).astype(BF16)

    def out_chunk(rws, mixed_src, c0):
        cols = slice(c0, c0 + MM_CHUNK)
        h_res = step_rows(t - 1, rws, cols, *h_refs[n_h // 2:])
        out_ref[rws, cols] = h_res + _dot(mixed_src[...], wout_ref[:, cols])

    def in_chunk(rws, proj_dst, c0):
        cols = slice(c0, c0 + MM_CHUNK)
        proj_dst[:, cols] = _dot(xn_ref[rws, :], win_ref[:, cols])

    def out_chunks(rws, mixed_src):
        return [functools.partial(out_chunk, rws, mixed_src, c0) for c0 in range(0, d_model, MM_CHUNK)]

    def in_chunks(rws, proj_dst):
        return [functools.partial(in_chunk, rws, proj_dst, c0) for c0 in range(0, IN_COLS, MM_CHUNK)]

    def half_step(work, blk_mix, trig, proj_mix, mixed_mix):
        progress = {"ticks": 0, "done": 0}

        def tick():
            progress["ticks"] += 1
            target = min(len(work), -(-len(work) * progress["ticks"] // MIX_TICKS))
            while progress["done"] < target:
                work[progress["done"]]()
                progress["done"] += 1

        _mix_tile(layer, blk_mix, sinks_ref, trig[0], trig[1], lbl_ref, qg_ref, kg_ref, hg_ref,
                  pw_ref, ps_ref, mstack_ref, lmask_ref, proj_mix, mixed_mix, expo_ref, st_ref,
                  kbuf_ref, ksw_ref, vbuf_ref, vsw_ref, uext_ref, tick)
        assert progress["done"] == len(work), "MIX_TICKS must match the tick() calls in _mix_tile"

    trig_a, trig_b = (cosa_ref, sina_ref), (cosb_ref, sinb_ref)

    @pl.when(t == 0)
    def _first_step():
        norm_rows()
        for chunk in in_chunks(rows_b, projb_ref):
            chunk()

    @pl.when((t > 0) & (t < last_step))
    def _main_step():
        norm_rows()
        half_step(out_chunks(rows_a, mixeda_ref) + in_chunks(rows_a, proja_ref), 2 * t - 2, trig_b,
                  projb_ref, mixedb_ref)
        half_step(out_chunks(rows_b, mixedb_ref) + in_chunks(rows_b, projb_ref), 2 * t - 1, trig_a,
                  proja_ref, mixeda_ref)

    @pl.when(t == last_step)
    def _drain_step():
        half_step(out_chunks(rows_a, mixeda_ref), 2 * t - 2, trig_b, projb_ref, mixedb_ref)
        for chunk in out_chunks(rows_b, mixedb_ref):
            chunk()


def _layer_call(layer, first, last, h_in, meta, cos, sin, sinks, norm_g, w_in, lb_logits, qg, kg, hg,
                pool_w, pool_scale, w_out, mstack, lmask):
    d = h_in.shape[1]
    n_steps = cos.shape[0] // STEP

    def rows(width, lag, skip_front=False):
        off = lag + (1 if skip_front else 0)
        hi = n_steps - 1 - (1 if skip_front else 0)
        return pl.BlockSpec((STEP, width), lambda t: (jnp.clip(t - off, 0, hi), 0))

    def tile_rows(width, tile_off):
        return pl.BlockSpec((TILE, width), lambda t: (jnp.clip(2 * t + tile_off, 0, 2 * n_steps - 1), 0))

    def whole(shape, single=False):
        idx = lambda t: (0,) * len(shape)
        if single:
            return pl.BlockSpec(shape, idx, pipeline_mode=pl.Buffered(1))
        return pl.BlockSpec(shape, idx)

    in_specs = [pl.BlockSpec(memory_space=pltpu.SMEM)]
    args = [sinks]
    for lag in (0, 1):
        in_specs.append(rows(d, lag, skip_front=first))
        args.append(h_in)
        if first:
            in_specs.append(whole(meta.shape))
            args.append(meta)
    in_specs += [
        tile_rows(LANES, -1), tile_rows(LANES, -1),
        tile_rows(LANES, 0), tile_rows(LANES, 0),
        whole(norm_g.shape),
        pl.BlockSpec(memory_space=pl.ANY),
        whole(lb_logits.shape),
        whole(qg.shape), whole(kg.shape), whole(hg.shape),
        whole(pool_w.shape), whole(pool_scale.shape),
        pl.BlockSpec(memory_space=pl.ANY),
        whole(mstack.shape, single=True), whole(lmask.shape, single=True),
    ]
    args += [cos, sin, cos, sin, norm_g, w_in, lb_logits, qg, kg, hg, pool_w, pool_scale, w_out, mstack, lmask]
    out_rows = (n_steps - 1) * STEP if last else n_steps * STEP
    assert w_in.shape[1] % (2 * TILE) == 0 and w_out.shape[1] % (2 * TILE) == 0
    assert w_out.shape[2] <= IN_COLS
    scratch = [
        pltpu.VMEM(w_in.shape[1:], BF16),
        pltpu.VMEM(w_out.shape[1:], BF16),
        pltpu.SemaphoreType.DMA((2,)),
        pltpu.VMEM((STEP, d), BF16),
        pltpu.VMEM((TILE, IN_COLS), F32),
        pltpu.VMEM((TILE, IN_COLS), F32),
        pltpu.VMEM((TILE, MIX_WIDTH), BF16),
        pltpu.VMEM((TILE, MIX_WIDTH), BF16),
        pltpu.VMEM(((len(LEVELS) + 1) * TILE, A_WIDTH), F32),
        pltpu.VMEM((HA, DK_A, DK_A), F32),
        pltpu.VMEM((KBUF_ROWS, LANES), BF16),
        pltpu.VMEM((KBUF_ROWS, LANES), BF16),
        pltpu.VMEM((KBUF_ROWS, LANES), BF16),
        pltpu.VMEM((KBUF_ROWS, LANES), BF16),
        pltpu.VMEM((max(POOL_WINDOWS) + TILE, C_WIDTH), F32),
    ]
    return pl.pallas_call(
        functools.partial(_layer_kernel, layer, first),
        grid=(n_steps + 1,),
        in_specs=in_specs,
        out_specs=rows(d, 1, skip_front=last),
        out_shape=jax.ShapeDtypeStruct((out_rows, d), F32),
        scratch_shapes=scratch,
        compiler_params=pltpu.CompilerParams(
            dimension_semantics=("arbitrary",), vmem_limit_bytes=VMEM_LIMIT),
        name=f"hybrid_layer{layer}",
    )(*args)


def kernel(x, meta_tokens, lb_logits, norm_g, w_in, q_norm_g, k_norm_g, attn_sinks, hgrn_norm_g,
           pool_w, pool_scale, w_out):
    b, seq, d = x.shape
    depth = w_in.shape[0]
    assert b == 1 and seq % STEP == 0
    assert w_in.shape[2] == IN_COLS and w_out.shape[1] == MIX_WIDTH
    assert meta_tokens.shape[0] == N_META

    p = STEP + seq
    pos = (jnp.arange(p) - (STEP - N_META)).astype(F32)
    half = DH_B // 2
    inv = jnp.power(ROPE_THETA, -jnp.arange(half, dtype=F32) * 2.0 / DH_B)
    ang = pos[:, None] * inv[None, :]
    cos = jnp.tile(jnp.cos(ang), (1, LANES // half))
    sin = jnp.tile(jnp.concatenate([-jnp.sin(ang), jnp.sin(ang)], axis=1), (1, LANES // DH_B))

    mstack_np, lmask_np = _decay_constants()
    mstack = jnp.asarray(mstack_np, BF16)
    lmask = jnp.asarray(lmask_np, F32)

    h = x[0]
    for l in range(depth):
        h = _layer_call(
            l, l == 0, l == depth - 1, h, meta_tokens.astype(F32), cos, sin, attn_sinks[l],
            norm_g[l][None, :], w_in, lb_logits,
            jnp.tile(q_norm_g[l], LANES // DH_B)[None, :], jnp.tile(k_norm_g[l], LANES // DH_B)[None, :],
            hgrn_norm_g[l][None, :], pool_w[l].astype(BF16), pool_scale[l][None, :],
            w_out, mstack, lmask)
    return h[None]
```

```python
import functools

import numpy as np
import jax
import jax.numpy as jnp
from jax import lax
from jax.experimental import pallas as pl
from jax.experimental.pallas import tpu as pltpu

F32 = jnp.float32
BF16 = jnp.bfloat16

N_META = 16
TILE = 128
PAD_FRONT = TILE - N_META
STEP = 2 * TILE
RMS_EPS = 1e-6
NEG_INF = -1e30
LOG_FLOOR = 1e-30
ROPE_THETA = 10000.0
LOG2E = 1.4426950408889634

HA, DK_A = 4, 128
HGRN_GROUP = 2
A_WIDTH = HA * DK_A
HB, KVH_B, DH_B = 16, 2, 64
B_WIDTH = HB * DH_B
KV_WIDTH = KVH_B * DH_B
POOL_WINDOWS = (2, 4, 8, 16)
CG_C = 128
C_WIDTH = len(POOL_WINDOWS) * CG_C
MIX_WIDTH = A_WIDTH + B_WIDTH + C_WIDTH

OFF_QA = 0
OFF_FA = OFF_QA + A_WIDTH
OFF_IA = OFF_FA + A_WIDTH
OFF_GA = OFF_IA + A_WIDTH
OFF_QB = OFF_GA + A_WIDTH
OFF_KB = OFF_QB + B_WIDTH
OFF_VB = OFF_KB + KV_WIDTH
OFF_GB = OFF_VB + KV_WIDTH
OFF_UC = OFF_GB + B_WIDTH
OFF_GC = OFF_UC + C_WIDTH
IN_COLS = OFF_GC + C_WIDTH
MM_CHUNK = 256

LANES = 128
QB = 64
KWIN = 256
KB_META0 = KWIN - N_META - TILE - QB
KB_PREV = KB_META0 + N_META
KB_CUR = KB_PREV + TILE
KB_META1 = KB_CUR + TILE
KBUF_ROWS = KB_PREV + QB + KWIN
assert KB_META0 >= 16 and KBUF_ROWS - KB_META1 - N_META >= 16
LEVELS = (128, 64, 32, 16, 8, 4, 2)
MIX_TICKS = 1 + HA + 2 + 2 * (TILE // QB) + len(POOL_WINDOWS)
VMEM_LIMIT = 60 * 1024 * 1024


def _decay_constants():
    r = np.arange(TILE)
    mats, masks = [], []
    for blk in LEVELS:
        half = blk // 2
        pos = r % blk
        piv = (r // blk) * blk + half - 1
        m = np.zeros((TILE, TILE), np.float32)
        for i in range(TILE):
            if pos[i] >= half:
                m[i, piv[i] + 1:i + 1] = 1.0
            else:
                m[i, i + 1:piv[i] + 1] = 1.0
        mats.append(m)
        same = (r[:, None] // blk) == (r[None, :] // blk)
        masks.append((same & (pos[:, None] >= half) & (pos[None, :] < half)).astype(np.float32))
    mats.append(np.tril(np.ones((TILE, TILE), np.float32)))
    masks.append(np.eye(TILE, dtype=np.float32))
    stack = np.concatenate(mats, axis=0)
    return np.concatenate([stack, stack], axis=1), np.stack(masks, axis=0)


def _dot(a, b):
    return jnp.dot(a, b, preferred_element_type=F32)


def _dot_nt(a, b):
    return lax.dot_general(a, b, (((1,), (1,)), ((), ())), preferred_element_type=F32)


def _dot_tn(a, b):
    return lax.dot_general(a, b, (((0,), (0,)), ((), ())), preferred_element_type=F32)


def _sigmoid(x):
    return 0.5 * jnp.tanh(0.5 * x) + 0.5


def _silu(x):
    t = 0.5 * x
    return t * jnp.tanh(t) + t


def _mix_tile(layer, blk, sinks_ref, cos_ref, sin_ref, lbl_ref, qg_ref, kg_ref, hg_ref,
              pw_ref, ps_ref, mstack_ref, lmask_ref, proj_ref, mixed_ref, expo_ref, st_ref,
              kbuf_ref, ksw_ref, vbuf_ref, vsw_ref, uext_ref, tick):
    rloc = lax.broadcasted_iota(jnp.int32, (TILE, 1), 0)
    row = blk * TILE + rloc
    valid = (row >= PAD_FRONT).astype(F32)

    rows = [lbl_ref[j:j + 1, :] for j in range(lbl_ref.shape[0])]
    mx = functools.reduce(jnp.maximum, rows)
    es = [jnp.exp(x - mx) for x in rows]
    lb = sum(es[1:layer + 1], jnp.zeros_like(mx)) / sum(es)

    z = proj_ref[:, OFF_FA:OFF_FA + A_WIDTH]
    sig = _sigmoid(z)
    f = lb + (1.0 - lb) * sig
    logf = jnp.log2(jnp.maximum(f, LOG_FLOOR)) * valid
    kk = (1.0 - lb) * (1.0 - sig) * valid
    proj_ref[:, OFF_QA:OFF_QA + A_WIDTH] = _silu(proj_ref[:, OFF_QA:OFF_QA + A_WIDTH])
    proj_ref[:, OFF_FA:OFF_FA + A_WIDTH] = kk
    lf_hi = logf.astype(BF16)
    lf_lo = (logf - lf_hi.astype(F32)).astype(BF16)
    expo_ref[...] = _dot(mstack_ref[...], jnp.concatenate([lf_hi, lf_lo], axis=0))
    n_lv = len(LEVELS)
    tick()

    def hgrn_heads(hds):
        cs = {hd: slice(hd * DK_A, (hd + 1) * DK_A) for hd in hds}
        qf_h = {hd: proj_ref[:, OFF_QA + hd * DK_A:OFF_QA + (hd + 1) * DK_A] for hd in hds}
        kk_h = {hd: proj_ref[:, OFF_FA + hd * DK_A:OFF_FA + (hd + 1) * DK_A] for hd in hds}
        attn = {hd: _dot_nt(qf_h[hd].astype(BF16), kk_h[hd].astype(BF16)) * lmask_ref[n_lv] for hd in hds}
        for li, bsz in enumerate(LEVELS):
            upper = (rloc & (bsz - 1)) >= (bsz // 2)
            for hd in hds:
                xb = (jnp.where(upper, qf_h[hd], kk_h[hd])
                      * jnp.exp2(expo_ref[li * TILE:(li + 1) * TILE, cs[hd]])).astype(BF16)
                attn[hd] = attn[hd] + _dot_nt(xb, xb) * lmask_ref[li]
            if li == len(LEVELS) // 2:
                tick()
        for hd in hds:
            v_h = proj_ref[:, OFF_IA + hd * DK_A:OFF_IA + (hd + 1) * DK_A].astype(BF16)
            o = _dot(attn[hd].astype(BF16), v_h)
            st = st_ref[hd]
            g_inc = expo_ref[n_lv * TILE:(n_lv + 1) * TILE, cs[hd]]
            q_dec = (qf_h[hd] * jnp.exp2(g_inc)).astype(BF16)
            o = o + _dot_nt(q_dec, st.astype(BF16))
            g_tot = g_inc[TILE - 1:TILE]
            k_dec = (kk_h[hd] * jnp.exp2(g_tot - g_inc)).astype(BF16)
            a_tot = jnp.exp2(g_tot)
            st_ref[hd] = a_tot * st + _dot_tn(v_h, k_dec)
            ya = o * lax.rsqrt(jnp.mean(o * o, axis=-1, keepdims=True) + RMS_EPS) * hg_ref[...]
            ga = proj_ref[:, OFF_GA + hd * DK_A:OFF_GA + (hd + 1) * DK_A]
            mixed_ref[:, hd * DK_A:(hd + 1) * DK_A] = (ya * _silu(ga)).astype(BF16)
        for _ in range(len(hds) - 1):
            tick()

    for hd0 in range(0, HA, HGRN_GROUP):
        hgrn_heads(tuple(range(hd0, hd0 + HGRN_GROUP)))

    lane = lax.broadcasted_iota(jnp.int32, (1, LANES), 1)
    lo_half = lane < DH_B
    first = (lane & (DH_B - 1)) < DH_B // 2
    cos = cos_ref[...]
    sin = sin_ref[...]

    def norm_rope(x, g):
        sq = x * x
        s0 = jnp.sum(jnp.where(lo_half, sq, 0.0), axis=-1, keepdims=True)
        s1 = jnp.sum(jnp.where(lo_half, 0.0, sq), axis=-1, keepdims=True)
        msq = jnp.where(lo_half, s0, s1) * (1.0 / DH_B)
        y = x * lax.rsqrt(msq + RMS_EPS) * g
        rot = jnp.where(first, pltpu.roll(y, LANES - DH_B // 2, axis=1),
                        pltpu.roll(y, DH_B // 2, axis=1))
        return y * cos + rot * sin

    kc = norm_rope(proj_ref[:, OFF_KB:OFF_KB + KV_WIDTH], kg_ref[...])
    vc = proj_ref[:, OFF_VB:OFF_VB + KV_WIDTH]
    is_meta_tile = blk == 0
    for buf, val in ((kbuf_ref, kc), (ksw_ref, pltpu.roll(kc, DH_B, axis=1)),
                     (vbuf_ref, vc), (vsw_ref, pltpu.roll(vc, DH_B, axis=1))):
        val = val.astype(BF16)
        buf[KB_CUR:KB_CUR + TILE, :] = val
        for m0 in (KB_META0, KB_META1):
            buf[m0:m0 + N_META, :] = jnp.where(is_meta_tile, val[PAD_FRONT:], buf[m0:m0 + N_META, :])
    tick()

    scale = DH_B ** -0.5 * LOG2E
    qcols = []
    for j in range(HB // 2):
        qj = norm_rope(proj_ref[:, OFF_QB + j * LANES:OFF_QB + (j + 1) * LANES], qg_ref[...]) * scale
        qcols.append((jnp.where(lo_half, qj, 0.0).astype(BF16), jnp.where(lo_half, 0.0, qj).astype(BF16)))
    tick()

    grp = HB // KVH_B
    heads_same = [hh for hh in range(HB) if (hh % 2) == (hh // grp)]
    heads_swap = [hh for hh in range(HB) if (hh % 2) != (hh // grp)]
    col = lax.broadcasted_iota(jnp.int32, (1, KWIN), 1)
    big = jnp.int32(1 << 30)
    windows = ((0, 0), (KBUF_ROWS - KWIN, KWIN - 1))

    def attn_sub_block(sb):
        w0, sink_col = windows[sb]
        q0 = sb * QB
        brow = w0 + col
        in_meta0 = (brow >= KB_META0) & (brow < KB_PREV)
        in_meta1 = (brow >= KB_META1) & (brow < KB_META1 + N_META)
        in_band = (brow >= KB_PREV) & (brow < KB_META1)
        kj = jnp.where(in_meta0, PAD_FRONT + brow - KB_META0,
                       jnp.where(in_meta1, PAD_FRONT + brow - KB_META1,
                                 jnp.where(in_band, (blk - 1) * TILE + brow - KB_PREV, big)))
        qi = blk * TILE + q0 + lax.broadcasted_iota(jnp.int32, (QB, 1), 0)
        allowed = (kj <= qi) & (in_meta0 | in_meta1 | ((kj > qi - TILE) & (kj >= TILE)))

        groups = ((heads_same, kbuf_ref, vbuf_ref), (heads_swap, ksw_ref, vsw_ref))
        s_all = []
        for heads, k_ref, _ in groups:
            lhs = jnp.concatenate([qcols[hh // 2][hh % 2][q0:q0 + QB] for hh in heads], axis=0)
            s_all.append(_dot_nt(lhs, k_ref[w0:w0 + KWIN, :]))
        ps, inv_den = [[] for _ in groups], [[] for _ in groups]
        for n in range(len(heads_same)):
            for g, (heads, _, _) in enumerate(groups):
                fill = jnp.where(col == sink_col, sinks_ref[heads[n]] * LOG2E, NEG_INF)
                s = jnp.where(allowed, s_all[g][n * QB:(n + 1) * QB], fill)
                pr = jnp.exp2(s - jnp.max(s, axis=-1, keepdims=True))
                inv_den[g].append(1.0 / jnp.sum(pr, axis=-1, keepdims=True))
                ps[g].append(pr.astype(BF16))
            if n == len(heads_same) // 2 - 1:
                tick()
        outs = {}
        for g, (heads, _, v_ref) in enumerate(groups):
            o_all = _dot(jnp.concatenate(ps[g], axis=0), v_ref[w0:w0 + KWIN, :])
            for n, hh in enumerate(heads):
                outs[hh] = o_all[n * QB:(n + 1) * QB] * inv_den[g][n]
        tick()
        for j in range(HB // 2):
            yb = jnp.where(lo_half, outs[2 * j], outs[2 * j + 1])
            gb = proj_ref[q0:q0 + QB, OFF_GB + j * LANES:OFF_GB + (j + 1) * LANES]
            mixed_ref[q0:q0 + QB, A_WIDTH + j * LANES:A_WIDTH + (j + 1) * LANES] = (
                yb * _silu(gb)).astype(BF16)

    for sb in range(len(windows)):
        attn_sub_block(sb)

    for buf in (kbuf_ref, ksw_ref, vbuf_ref, vsw_ref):
        buf[KB_PREV:KB_CUR, :] = buf[KB_CUR:KB_CUR + TILE, :]

    wmax = max(POOL_WINDOWS)
    ug = proj_ref[:, OFF_UC:OFF_UC + C_WIDTH] * valid
    uext_ref[wmax:wmax + TILE, :] = ug
    n_valid = jnp.maximum(row - (PAD_FRONT - 1), 0)
    for g, w in enumerate(POOL_WINDOWS):
        cs = slice(g * CG_C, (g + 1) * CG_C)
        acc = uext_ref[wmax:wmax + TILE, cs]
        for s in range(1, w):
            acc = acc + uext_ref[wmax - s:wmax - s + TILE, cs]
        cnt = (n_valid - jnp.maximum(row - w - (PAD_FRONT - 1), 0)).astype(F32)
        pooled = (acc / jnp.maximum(cnt, 1.0) - ug[:, cs]) * valid
        y = _dot(pooled.astype(BF16), pw_ref[g]) * ps_ref[:, cs]
        gc = proj_ref[:, OFF_GC + g * CG_C:OFF_GC + (g + 1) * CG_C]
        mixed_ref[:, A_WIDTH + B_WIDTH + g * CG_C:A_WIDTH + B_WIDTH + (g + 1) * CG_C] = (
            y * _silu(gc)).astype(BF16)
        tick()
    uext_ref[0:wmax, :] = ug[TILE - wmax:]


def _layer_kernel(layer, first, sinks_ref, *refs):
    n_h = 4 if first else 2
    h_refs, refs = refs[:n_h], refs[n_h:]
    (cosb_ref, sinb_ref, cosa_ref, sina_ref, ng_ref, win_hbm, lbl_ref, qg_ref, kg_ref, hg_ref, pw_ref,
     ps_ref, wout_hbm, mstack_ref, lmask_ref, out_ref, win_ref, wout_ref, wsem, xn_ref, proja_ref,
     projb_ref, mixeda_ref, mixedb_ref, expo_ref, st_ref, kbuf_ref, ksw_ref, vbuf_ref, vsw_ref,
     uext_ref) = refs
    t = pl.program_id(0)

    @pl.when(t == 0)
    def _load_weights_and_zero_state():
        stage = (proja_ref, projb_ref)
        for w_hbm, w_vmem in ((win_hbm, win_ref), (wout_hbm, wout_ref)):
            width = w_vmem.shape[1]
            n_chunks = w_vmem.shape[0] // TILE

            def chunk_copy(k, slot, w_hbm=w_hbm, width=width):
                return pltpu.make_async_copy(w_hbm.at[layer, pl.ds(k * TILE, TILE), :],
                                             stage[slot].at[:, 0:width], wsem.at[slot])

            chunk_copy(0, 0).start()
            chunk_copy(1, 1).start()

            def pair(i, carry, w_vmem=w_vmem, width=width, n_chunks=n_chunks, chunk_copy=chunk_copy):
                for slot in (0, 1):
                    k = 2 * i + slot
                    chunk_copy(k, slot).wait()
                    w_vmem[pl.ds(pl.multiple_of(k * TILE, TILE), TILE), :] = stage[slot][:, 0:width].astype(BF16)

                    @pl.when(k + 2 < n_chunks)
                    def _next():
                        chunk_copy(k + 2, slot).start()
                return carry

            lax.fori_loop(0, n_chunks // 2, pair, 0)

        for ref in (proja_ref, projb_ref, mixeda_ref, mixedb_ref, st_ref, kbuf_ref, ksw_ref,
                    vbuf_ref, vsw_ref, uext_ref):
            ref[...] = jnp.zeros_like(ref)

    def step_rows(step_idx, rws, cols, x_ref, meta_ref=None):
        x = x_ref[rws, cols]
        if not first:
            return x
        if rws.stop is not None and rws.stop < STEP:
            return jnp.where(step_idx == 0, 0.0, x)
        n_rows = x.shape[0]
        front = jnp.concatenate([jnp.zeros((n_rows - N_META, x.shape[1]), F32), meta_ref[:, cols]], axis=0)
        return jnp.where(step_idx == 0, front, x)

    d_model = out_ref.shape[1]
    last_step = pl.num_programs(0) - 1
    rows_a, rows_b = slice(0, TILE), slice(TILE, STEP)

    def norm_rows():
        h = step_rows(t, slice(None), slice(None), *h_refs[:n_h // 2])
        ms = jnp.mean(h * h, axis=-1, keepdims=True)
        xn_ref[...] = (h * lax.rsqrt(ms + RMS_EPS) * ng_ref[...]).astype(BF16)

    def out_chunk(rws, mixed_src, c0):
        cols = slice(c0, c0 + MM_CHUNK)
        h_res = step_rows(t - 1, rws, cols, *h_refs[n_h // 2:])
        out_ref[rws, cols] = h_res + _dot(mixed_src[...], wout_ref[:, cols])

    def in_chunk(rws, proj_dst, c0):
        cols = slice(c0, c0 + MM_CHUNK)
        proj_dst[:, cols] = _dot(xn_ref[rws, :], win_ref[:, cols])

    def out_chunks(rws, mixed_src):
        return [functools.partial(out_chunk, rws, mixed_src, c0) for c0 in range(0, d_model, MM_CHUNK)]

    def in_chunks(rws, proj_dst):
        return [functools.partial(in_chunk, rws, proj_dst, c0) for c0 in range(0, IN_COLS, MM_CHUNK)]

    def half_step(work, blk_mix, trig, proj_mix, mixed_mix):
        progress = {"ticks": 0, "done": 0}

        def tick():
            progress["ticks"] += 1
            target = min(len(work), -(-len(work) * progress["ticks"] // (MIX_TICKS + 1)))
            while progress["done"] < target:
                work[progress["done"]]()
                progress["done"] += 1

        tick()
        _mix_tile(layer, blk_mix, sinks_ref, trig[0], trig[1], lbl_ref, qg_ref, kg_ref, hg_ref,
                  pw_ref, ps_ref, mstack_ref, lmask_ref, proj_mix, mixed_mix, expo_ref, st_ref,
                  kbuf_ref, ksw_ref, vbuf_ref, vsw_ref, uext_ref, tick)
        assert progress["done"] == len(work), "MIX_TICKS must match the tick() calls in _mix_tile"

    trig_a, trig_b = (cosa_ref, sina_ref), (cosb_ref, sinb_ref)

    @pl.when(t == 0)
    def _first_step():
        norm_rows()
        for chunk in in_chunks(rows_b, projb_ref):
            chunk()

    @pl.when((t > 0) & (t < last_step))
    def _main_step():
        norm_rows()
        half_step(out_chunks(rows_a, mixeda_ref) + in_chunks(rows_a, proja_ref), 2 * t - 2, trig_b,
                  projb_ref, mixedb_ref)
        half_step(out_chunks(rows_b, mixedb_ref) + in_chunks(rows_b, projb_ref), 2 * t - 1, trig_a,
                  proja_ref, mixeda_ref)

    @pl.when(t == last_step)
    def _drain_step():
        half_step(out_chunks(rows_a, mixeda_ref), 2 * t - 2, trig_b, projb_ref, mixedb_ref)
        for chunk in out_chunks(rows_b, mixedb_ref):
            chunk()


def _layer_call(layer, first, last, h_in, meta, cos, sin, sinks, norm_g, w_in, lb_logits, qg, kg, hg,
                pool_w, pool_scale, w_out, mstack, lmask):
    d = h_in.shape[1]
    n_steps = cos.shape[0] // STEP

    def rows(width, lag, skip_front=False):
        off = lag + (1 if skip_front else 0)
        hi = n_steps - 1 - (1 if skip_front else 0)
        return pl.BlockSpec((STEP, width), lambda t: (jnp.clip(t - off, 0, hi), 0))

    def tile_rows(width, tile_off):
        return pl.BlockSpec((TILE, width), lambda t: (jnp.clip(2 * t + tile_off, 0, 2 * n_steps - 1), 0))

    def whole(shape, single=False):
        idx = lambda t: (0,) * len(shape)
        if single:
            return pl.BlockSpec(shape, idx, pipeline_mode=pl.Buffered(1))
        return pl.BlockSpec(shape, idx)

    in_specs = [pl.BlockSpec(memory_space=pltpu.SMEM)]
    args = [sinks]
    for lag in (0, 1):
        in_specs.append(rows(d, lag, skip_front=first))
        args.append(h_in)
        if first:
            in_specs.append(whole(meta.shape))
            args.append(meta)
    in_specs += [
        tile_rows(LANES, -1), tile_rows(LANES, -1),
        tile_rows(LANES, 0), tile_rows(LANES, 0),
        whole(norm_g.shape),
        pl.BlockSpec(memory_space=pl.ANY),
        whole(lb_logits.shape),
        whole(qg.shape), whole(kg.shape), whole(hg.shape),
        whole(pool_w.shape), whole(pool_scale.shape),
        pl.BlockSpec(memory_space=pl.ANY),
        whole(mstack.shape, single=True), whole(lmask.shape, single=True),
    ]
    args += [cos, sin, cos, sin, norm_g, w_in, lb_logits, qg, kg, hg, pool_w, pool_scale, w_out, mstack, lmask]
    out_rows = (n_steps - 1) * STEP if last else n_steps * STEP
    assert w_in.shape[1] % (2 * TILE) == 0 and w_out.shape[1] % (2 * TILE) == 0
    assert w_out.shape[2] <= IN_COLS
    scratch = [
        pltpu.VMEM(w_in.shape[1:], BF16),
        pltpu.VMEM(w_out.shape[1:], BF16),
        pltpu.SemaphoreType.DMA((2,)),
        pltpu.VMEM((STEP, d), BF16),
        pltpu.VMEM((TILE, IN_COLS), F32),
        pltpu.VMEM((TILE, IN_COLS), F32),
        pltpu.VMEM((TILE, MIX_WIDTH), BF16),
        pltpu.VMEM((TILE, MIX_WIDTH), BF16),
        pltpu.VMEM(((len(LEVELS) + 1) * TILE, A_WIDTH), F32),
        pltpu.VMEM((HA, DK_A, DK_A), F32),
        pltpu.VMEM((KBUF_ROWS, LANES), BF16),
        pltpu.VMEM((KBUF_ROWS, LANES), BF16),
        pltpu.VMEM((KBUF_ROWS, LANES), BF16),
        pltpu.VMEM((KBUF_ROWS, LANES), BF16),
        pltpu.VMEM((max(POOL_WINDOWS) + TILE, C_WIDTH), F32),
    ]
    return pl.pallas_call(
        functools.partial(_layer_kernel, layer, first),
        grid=(n_steps + 1,),
        in_specs=in_specs,
        out_specs=rows(d, 1, skip_front=last),
        out_shape=jax.ShapeDtypeStruct((out_rows, d), F32),
        scratch_shapes=scratch,
        compiler_params=pltpu.CompilerParams(
            dimension_semantics=("arbitrary",), vmem_limit_bytes=VMEM_LIMIT),
        name=f"hybrid_layer{layer}",
    )(*args)


def kernel(x, meta_tokens, lb_logits, norm_g, w_in, q_norm_g, k_norm_g, attn_sinks, hgrn_norm_g,
           pool_w, pool_scale, w_out):
    b, seq, d = x.shape
    depth = w_in.shape[0]
    assert b == 1 and seq % STEP == 0
    assert w_in.shape[2] == IN_COLS and w_out.shape[1] == MIX_WIDTH
    assert meta_tokens.shape[0] == N_META

    p = STEP + seq
    pos = (jnp.arange(p) - (STEP - N_META)).astype(F32)
    half = DH_B // 2
    inv = jnp.power(ROPE_THETA, -jnp.arange(half, dtype=F32) * 2.0 / DH_B)
    ang = pos[:, None] * inv[None, :]
    cos = jnp.tile(jnp.cos(ang), (1, LANES // half))
    sin = jnp.tile(jnp.concatenate([-jnp.sin(ang), jnp.sin(ang)], axis=1), (1, LANES // DH_B))

    mstack_np, lmask_np = _decay_constants()
    mstack = jnp.asarray(mstack_np, BF16)
    lmask = jnp.asarray(lmask_np, F32)

    h = x[0]
    for l in range(depth):
        h = _layer_call(
            l, l == 0, l == depth - 1, h, meta_tokens.astype(F32), cos, sin, attn_sinks[l],
            norm_g[l][None, :], w_in, lb_logits,
            jnp.tile(q_norm_g[l], LANES // DH_B)[None, :], jnp.tile(k_norm_g[l], LANES // DH_B)[None, :],
            hgrn_norm_g[l][None, :], pool_w[l].astype(BF16), pool_scale[l][None, :],
            w_out, mstack, lmask)
    return h[None]
```

```python
import functools

import numpy as np
import jax
import jax.numpy as jnp
from jax import lax
from jax.experimental import pallas as pl
from jax.experimental.pallas import tpu as pltpu

F32 = jnp.float32
BF16 = jnp.bfloat16

N_META = 16
TILE = 128
PAD_FRONT = TILE - N_META
STEP = 2 * TILE
RMS_EPS = 1e-6
NEG_INF = -1e30
LOG_FLOOR = 1e-30
ROPE_THETA = 10000.0
LOG2E = 1.4426950408889634

HA, DK_A = 4, 128
HGRN_GROUP = 2
A_WIDTH = HA * DK_A
HB, KVH_B, DH_B = 16, 2, 64
B_WIDTH = HB * DH_B
KV_WIDTH = KVH_B * DH_B
POOL_WINDOWS = (2, 4, 8, 16)
CG_C = 128
C_WIDTH = len(POOL_WINDOWS) * CG_C
MIX_WIDTH = A_WIDTH + B_WIDTH + C_WIDTH

OFF_QA = 0
OFF_FA = OFF_QA + A_WIDTH
OFF_IA = OFF_FA + A_WIDTH
OFF_GA = OFF_IA + A_WIDTH
OFF_QB = OFF_GA + A_WIDTH
OFF_KB = OFF_QB + B_WIDTH
OFF_VB = OFF_KB + KV_WIDTH
OFF_GB = OFF_VB + KV_WIDTH
OFF_UC = OFF_GB + B_WIDTH
OFF_GC = OFF_UC + C_WIDTH
IN_COLS = OFF_GC + C_WIDTH
MM_CHUNK = 256

LANES = 128
QB = 64
KWIN = 256
KB_META0 = KWIN - N_META - TILE - QB
KB_PREV = KB_META0 + N_META
KB_CUR = KB_PREV + TILE
KB_META1 = KB_CUR + TILE
KBUF_ROWS = KB_PREV + QB + KWIN
assert KB_META0 >= 16 and KBUF_ROWS - KB_META1 - N_META >= 16
LEVELS = (128, 64, 32, 16, 8, 4, 2)
SUBLANES = 8
MM_LEVELS = tuple(b for b in LEVELS if 2 < b < 2 * SUBLANES)
EXPO_BLOCKS = len(MM_LEVELS) + 2
MIX_TICKS = 1 + HA + 2 + 2 * (TILE // QB) + len(POOL_WINDOWS)
VMEM_LIMIT = 60 * 1024 * 1024


def _decay_constants():
    r = np.arange(TILE)
    mats, masks = [], []
    for blk in LEVELS:
        half = blk // 2
        pos = r % blk
        piv = (r // blk) * blk + half - 1
        m = np.zeros((TILE, TILE), np.float32)
        for i in range(TILE):
            if pos[i] >= half:
                m[i, piv[i] + 1:i + 1] = 1.0
            else:
                m[i, i + 1:piv[i] + 1] = 1.0
        if blk in MM_LEVELS:
            mats.append(m)
        same = (r[:, None] // blk) == (r[None, :] // blk)
        masks.append((same & (pos[:, None] >= half) & (pos[None, :] < half)).astype(np.float32))
    mats.append(np.tril(np.ones((TILE, TILE), np.float32)))
    masks.append(np.eye(TILE, dtype=np.float32))
    stack = np.concatenate(mats, axis=0)
    return np.concatenate([stack, stack], axis=1), np.stack(masks, axis=0)


def _dot(a, b):
    return jnp.dot(a, b, preferred_element_type=F32)


def _dot_nt(a, b):
    return lax.dot_general(a, b, (((1,), (1,)), ((), ())), preferred_element_type=F32)


def _dot_tn(a, b):
    return lax.dot_general(a, b, (((0,), (0,)), ((), ())), preferred_element_type=F32)


def _sigmoid(x):
    return 0.5 * jnp.tanh(0.5 * x) + 0.5


def _silu(x):
    t = 0.5 * x
    return t * jnp.tanh(t) + t


def _mix_tile(layer, blk, sinks_ref, cos_ref, sin_ref, lbl_ref, qg_ref, kg_ref, hg_ref,
              pw_ref, ps_ref, mstack_ref, lmask_ref, proj_ref, mixed_ref, expo_ref, st_ref,
              kbuf_ref, ksw_ref, vbuf_ref, vsw_ref, uext_ref, tick):
    rloc = lax.broadcasted_iota(jnp.int32, (TILE, 1), 0)
    row = blk * TILE + rloc
    valid = (row >= PAD_FRONT).astype(F32)

    rows = [lbl_ref[j:j + 1, :] for j in range(lbl_ref.shape[0])]
    mx = functools.reduce(jnp.maximum, rows)
    es = [jnp.exp(x - mx) for x in rows]
    lb = sum(es[1:layer + 1], jnp.zeros_like(mx)) / sum(es)

    z = proj_ref[:, OFF_FA:OFF_FA + A_WIDTH]
    sig = _sigmoid(z)
    f = lb + (1.0 - lb) * sig
    logf = jnp.log2(jnp.maximum(f, LOG_FLOOR)) * valid
    kk = (1.0 - lb) * (1.0 - sig) * valid
    proj_ref[:, OFF_QA:OFF_QA + A_WIDTH] = _silu(proj_ref[:, OFF_QA:OFF_QA + A_WIDTH])
    proj_ref[:, OFF_FA:OFF_FA + A_WIDTH] = kk
    lf_hi = logf.astype(BF16)
    lf_lo = (logf - lf_hi.astype(F32)).astype(BF16)
    n_mm = len(MM_LEVELS) + 1
    expo_ref[0:n_mm * TILE, :] = _dot(mstack_ref[...], jnp.concatenate([lf_hi, lf_lo], axis=0))
    expo_ref[n_mm * TILE:(n_mm + 1) * TILE, :] = jnp.where(valid > 0.0, jnp.maximum(f, LOG_FLOOR), 1.0)
    n_lv = len(LEVELS)
    tick()

    def level_factors(bsz, g, f_row, cols):
        half = bsz // 2
        if bsz in MM_LEVELS:
            i = MM_LEVELS.index(bsz)
            return jnp.exp2(expo_ref[i * TILE:(i + 1) * TILE, cols])
        if bsz == 2:
            return jnp.where((rloc & 1) == 1, f_row, 1.0)
        parts = []
        for b0 in range(0, TILE, bsz):
            piv = jnp.broadcast_to(g[b0 + half - 1:b0 + half, :], (half, g.shape[1]))
            parts += [piv - g[b0:b0 + half], g[b0 + half:b0 + bsz] - piv]
        return jnp.exp2(jnp.concatenate(parts, axis=0))

    def hgrn_heads(hds):
        cs = {hd: slice(hd * DK_A, (hd + 1) * DK_A) for hd in hds}
        qf_h = {hd: proj_ref[:, OFF_QA + hd * DK_A:OFF_QA + (hd + 1) * DK_A] for hd in hds}
        kk_h = {hd: proj_ref[:, OFF_FA + hd * DK_A:OFF_FA + (hd + 1) * DK_A] for hd in hds}
        attn = {hd: _dot_nt(qf_h[hd].astype(BF16), kk_h[hd].astype(BF16)) * lmask_ref[n_lv] for hd in hds}
        g_inc = {hd: expo_ref[(n_mm - 1) * TILE:n_mm * TILE, cs[hd]] for hd in hds}
        f_row = {hd: expo_ref[n_mm * TILE:(n_mm + 1) * TILE, cs[hd]] for hd in hds}
        for li, bsz in enumerate(LEVELS):
            upper = (rloc & (bsz - 1)) >= (bsz // 2)
            for hd in hds:
                xb = (jnp.where(upper, qf_h[hd], kk_h[hd])
                      * level_factors(bsz, g_inc[hd], f_row[hd], cs[hd])).astype(BF16)
                attn[hd] = attn[hd] + _dot_nt(xb, xb) * lmask_ref[li]
            if li == len(LEVELS) // 2:
                tick()
        v_h = {hd: proj_ref[:, OFF_IA + hd * DK_A:OFF_IA + (hd + 1) * DK_A].astype(BF16) for hd in hds}
        st = {hd: st_ref[hd] for hd in hds}
        o = {hd: _dot(attn[hd].astype(BF16), v_h[hd]) for hd in hds}
        for hd in hds:
            q_dec = (qf_h[hd] * jnp.exp2(g_inc[hd])).astype(BF16)
            o[hd] = o[hd] + _dot_nt(q_dec, st[hd].astype(BF16))
        for hd in hds:
            g_tot = g_inc[hd][TILE - 1:TILE]
            k_dec = (kk_h[hd] * jnp.exp2(g_tot - g_inc[hd])).astype(BF16)
            st_ref[hd] = jnp.exp2(g_tot) * st[hd] + _dot_tn(v_h[hd], k_dec)
        for _ in range(len(hds) - 1):
            tick()
        for hd in hds:
            ya = o[hd] * lax.rsqrt(jnp.mean(o[hd] * o[hd], axis=-1, keepdims=True) + RMS_EPS) * hg_ref[...]
            ga = proj_ref[:, OFF_GA + hd * DK_A:OFF_GA + (hd + 1) * DK_A]
            mixed_ref[:, hd * DK_A:(hd + 1) * DK_A] = (ya * _silu(ga)).astype(BF16)

    for hd0 in range(0, HA, HGRN_GROUP):
        hgrn_heads(tuple(range(hd0, hd0 + HGRN_GROUP)))

    lane = lax.broadcasted_iota(jnp.int32, (1, LANES), 1)
    lo_half = lane < DH_B
    first = (lane & (DH_B - 1)) < DH_B // 2
    cos = cos_ref[...]
    sin = sin_ref[...]

    def norm_rope(x, g):
        sq = x * x
        s0 = jnp.sum(jnp.where(lo_half, sq, 0.0), axis=-1, keepdims=True)
        s1 = jnp.sum(jnp.where(lo_half, 0.0, sq), axis=-1, keepdims=True)
        msq = jnp.where(lo_half, s0, s1) * (1.0 / DH_B)
        y = x * lax.rsqrt(msq + RMS_EPS) * g
        rot = jnp.where(first, pltpu.roll(y, LANES - DH_B // 2, axis=1),
                        pltpu.roll(y, DH_B // 2, axis=1))
        return y * cos + rot * sin

    kc = norm_rope(proj_ref[:, OFF_KB:OFF_KB + KV_WIDTH], kg_ref[...])
    vc = proj_ref[:, OFF_VB:OFF_VB + KV_WIDTH]
    is_meta_tile = blk == 0
    for buf, val in ((kbuf_ref, kc), (ksw_ref, pltpu.roll(kc, DH_B, axis=1)),
                     (vbuf_ref, vc), (vsw_ref, pltpu.roll(vc, DH_B, axis=1))):
        val = val.astype(BF16)
        buf[KB_CUR:KB_CUR + TILE, :] = val
        for m0 in (KB_META0, KB_META1):
            buf[m0:m0 + N_META, :] = jnp.where(is_meta_tile, val[PAD_FRONT:], buf[m0:m0 + N_META, :])
    tick()

    scale = DH_B ** -0.5 * LOG2E
    qcols = []
    for j in range(HB // 2):
        qj = norm_rope(proj_ref[:, OFF_QB + j * LANES:OFF_QB + (j + 1) * LANES], qg_ref[...]) * scale
        qcols.append((jnp.where(lo_half, qj, 0.0).astype(BF16), jnp.where(lo_half, 0.0, qj).astype(BF16)))
    tick()

    grp = HB // KVH_B
    heads_same = [hh for hh in range(HB) if (hh % 2) == (hh // grp)]
    heads_swap = [hh for hh in range(HB) if (hh % 2) != (hh // grp)]
    col = lax.broadcasted_iota(jnp.int32, (1, KWIN), 1)
    big = jnp.int32(1 << 30)
    windows = ((0, 0), (KBUF_ROWS - KWIN, KWIN - 1))

    def attn_sub_block(sb):
        w0, sink_col = windows[sb]
        q0 = sb * QB
        brow = w0 + col
        in_meta0 = (brow >= KB_META0) & (brow < KB_PREV)
        in_meta1 = (brow >= KB_META1) & (brow < KB_META1 + N_META)
        in_band = (brow >= KB_PREV) & (brow < KB_META1)
        kj = jnp.where(in_meta0, PAD_FRONT + brow - KB_META0,
                       jnp.where(in_meta1, PAD_FRONT + brow - KB_META1,
                                 jnp.where(in_band, (blk - 1) * TILE + brow - KB_PREV, big)))
        qi = blk * TILE + q0 + lax.broadcasted_iota(jnp.int32, (QB, 1), 0)
        allowed = (kj <= qi) & (in_meta0 | in_meta1 | ((kj > qi - TILE) & (kj >= TILE)))

        groups = ((heads_same, kbuf_ref, vbuf_ref), (heads_swap, ksw_ref, vsw_ref))
        s_all = []
        for heads, k_ref, _ in groups:
            lhs = jnp.concatenate([qcols[hh // 2][hh % 2][q0:q0 + QB] for hh in heads], axis=0)
            s_all.append(_dot_nt(lhs, k_ref[w0:w0 + KWIN, :]))
        ps, inv_den = [[] for _ in groups], [[] for _ in groups]
        for n in range(len(heads_same)):
            for g, (heads, _, _) in enumerate(groups):
                fill = jnp.where(col == sink_col, sinks_ref[heads[n]] * LOG2E, NEG_INF)
                s = jnp.where(allowed, s_all[g][n * QB:(n + 1) * QB], fill)
                pr = jnp.exp2(s - jnp.max(s, axis=-1, keepdims=True))
                inv_den[g].append(1.0 / jnp.sum(pr, axis=-1, keepdims=True))
                ps[g].append(pr.astype(BF16))
            if n == len(heads_same) // 2 - 1:
                tick()
        outs = {}
        for g, (heads, _, v_ref) in enumerate(groups):
            o_all = _dot(jnp.concatenate(ps[g], axis=0), v_ref[w0:w0 + KWIN, :])
            for n, hh in enumerate(heads):
                outs[hh] = o_all[n * QB:(n + 1) * QB] * inv_den[g][n]
        tick()
        for j in range(HB // 2):
            yb = jnp.where(lo_half, outs[2 * j], outs[2 * j + 1])
            gb = proj_ref[q0:q0 + QB, OFF_GB + j * LANES:OFF_GB + (j + 1) * LANES]
            mixed_ref[q0:q0 + QB, A_WIDTH + j * LANES:A_WIDTH + (j + 1) * LANES] = (
                yb * _silu(gb)).astype(BF16)

    for sb in range(len(windows)):
        attn_sub_block(sb)

    for buf in (kbuf_ref, ksw_ref, vbuf_ref, vsw_ref):
        buf[KB_PREV:KB_CUR, :] = buf[KB_CUR:KB_CUR + TILE, :]

    wmax = max(POOL_WINDOWS)
    ug = proj_ref[:, OFF_UC:OFF_UC + C_WIDTH] * valid
    uext_ref[wmax:wmax + TILE, :] = ug
    n_valid = jnp.maximum(row - (PAD_FRONT - 1), 0)
    for g, w in enumerate(POOL_WINDOWS):
        cs = slice(g * CG_C, (g + 1) * CG_C)
        acc = uext_ref[wmax:wmax + TILE, cs]
        for s in range(1, w):
            acc = acc + uext_ref[wmax - s:wmax - s + TILE, cs]
        cnt = (n_valid - jnp.maximum(row - w - (PAD_FRONT - 1), 0)).astype(F32)
        pooled = (acc / jnp.maximum(cnt, 1.0) - ug[:, cs]) * valid
        y = _dot(pooled.astype(BF16), pw_ref[g]) * ps_ref[:, cs]
        gc = proj_ref[:, OFF_GC + g * CG_C:OFF_GC + (g + 1) * CG_C]
        mixed_ref[:, A_WIDTH + B_WIDTH + g * CG_C:A_WIDTH + B_WIDTH + (g + 1) * CG_C] = (
            y * _silu(gc)).astype(BF16)
        tick()
    uext_ref[0:wmax, :] = ug[TILE - wmax:]


def _layer_kernel(layer, first, sinks_ref, *refs):
    n_h = 4 if first else 2
    h_refs, refs = refs[:n_h], refs[n_h:]
    (cosb_ref, sinb_ref, cosa_ref, sina_ref, ng_ref, win_hbm, lbl_ref, qg_ref, kg_ref, hg_ref, pw_ref,
     ps_ref, wout_hbm, mstack_ref, lmask_ref, out_ref, win_ref, wout_ref, wsem, xn_ref, proja_ref,
     projb_ref, mixeda_ref, mixedb_ref, expo_ref, st_ref, kbuf_ref, ksw_ref, vbuf_ref, vsw_ref,
     uext_ref) = refs
    t = pl.program_id(0)

    @pl.when(t == 0)
    def _load_weights_and_zero_state():
        stage = (proja_ref, projb_ref)
        for w_hbm, w_vmem in ((win_hbm, win_ref), (wout_hbm, wout_ref)):
            width = w_vmem.shape[1]
            n_chunks = w_vmem.shape[0] // TILE

            def chunk_copy(k, slot, w_hbm=w_hbm, width=width):
                return pltpu.make_async_copy(w_hbm.at[layer, pl.ds(k * TILE, TILE), :],
                                             stage[slot].at[:, 0:width], wsem.at[slot])

            chunk_copy(0, 0).start()
            chunk_copy(1, 1).start()

            def pair(i, carry, w_vmem=w_vmem, width=width, n_chunks=n_chunks, chunk_copy=chunk_copy):
                for slot in (0, 1):
                    k = 2 * i + slot
                    chunk_copy(k, slot).wait()
                    w_vmem[pl.ds(pl.multiple_of(k * TILE, TILE), TILE), :] = stage[slot][:, 0:width].astype(BF16)

                    @pl.when(k + 2 < n_chunks)
                    def _next():
                        chunk_copy(k + 2, slot).start()
                return carry

            lax.fori_loop(0, n_chunks // 2, pair, 0)

        for ref in (proja_ref, projb_ref, mixeda_ref, mixedb_ref, st_ref, kbuf_ref, ksw_ref,
                    vbuf_ref, vsw_ref, uext_ref):
            ref[...] = jnp.zeros_like(ref)

    def step_rows(step_idx, rws, cols, x_ref, meta_ref=None):
        x = x_ref[rws, cols]
        if not first:
            return x
        if rws.stop is not None and rws.stop < STEP:
            return jnp.where(step_idx == 0, 0.0, x)
        n_rows = x.shape[0]
        front = jnp.concatenate([jnp.zeros((n_rows - N_META, x.shape[1]), F32), meta_ref[:, cols]], axis=0)
        return jnp.where(step_idx == 0, front, x)

    d_model = out_ref.shape[1]
    last_step = pl.num_programs(0) - 1
    rows_a, rows_b = slice(0, TILE), slice(TILE, STEP)

    def norm_rows():
        h = step_rows(t, slice(None), slice(None), *h_refs[:n_h // 2])
        ms = jnp.mean(h * h, axis=-1, keepdims=True)
        xn_ref[...] = (h * lax.rsqrt(ms + RMS_EPS) * ng_ref[...]).astype(BF16)

    def out_chunk(rws, mixed_src, c0):
        cols = slice(c0, c0 + MM_CHUNK)
        h_res = step_rows(t - 1, rws, cols, *h_refs[n_h // 2:])
        out_ref[rws, cols] = h_res + _dot(mixed_src[...], wout_ref[:, cols])

    def in_chunk(rws, proj_dst, c0):
        cols = slice(c0, c0 + MM_CHUNK)
        proj_dst[:, cols] = _dot(xn_ref[rws, :], win_ref[:, cols])

    def out_chunks(rws, mixed_src):
        return [functools.partial(out_chunk, rws, mixed_src, c0) for c0 in range(0, d_model, MM_CHUNK)]

    def in_chunks(rws, proj_dst):
        return [functools.partial(in_chunk, rws, proj_dst, c0) for c0 in range(0, IN_COLS, MM_CHUNK)]

    def half_step(work, blk_mix, trig, proj_mix, mixed_mix):
        progress = {"ticks": 0, "done": 0}

        def tick():
            progress["ticks"] += 1
            target = min(len(work), -(-len(work) * progress["ticks"] // (MIX_TICKS + 1)))
            while progress["done"] < target:
                work[progress["done"]]()
                progress["done"] += 1

        tick()
        _mix_tile(layer, blk_mix, sinks_ref, trig[0], trig[1], lbl_ref, qg_ref, kg_ref, hg_ref,
                  pw_ref, ps_ref, mstack_ref, lmask_ref, proj_mix, mixed_mix, expo_ref, st_ref,
                  kbuf_ref, ksw_ref, vbuf_ref, vsw_ref, uext_ref, tick)
        assert progress["done"] == len(work), "MIX_TICKS must match the tick() calls in _mix_tile"

    trig_a, trig_b = (cosa_ref, sina_ref), (cosb_ref, sinb_ref)

    @pl.when(t == 0)
    def _first_step():
        norm_rows()
        for chunk in in_chunks(rows_b, projb_ref):
            chunk()

    @pl.when((t > 0) & (t < last_step))
    def _main_step():
        norm_rows()
        half_step(out_chunks(rows_a, mixeda_ref) + in_chunks(rows_a, proja_ref), 2 * t - 2, trig_b,
                  projb_ref, mixedb_ref)
        half_step(out_chunks(rows_b, mixedb_ref) + in_chunks(rows_b, projb_ref), 2 * t - 1, trig_a,
                  proja_ref, mixeda_ref)

    @pl.when(t == last_step)
    def _drain_step():
        half_step(out_chunks(rows_a, mixeda_ref), 2 * t - 2, trig_b, projb_ref, mixedb_ref)
        for chunk in out_chunks(rows_b, mixedb_ref):
            chunk()


def _layer_call(layer, first, last, h_in, meta, cos, sin, sinks, norm_g, w_in, lb_logits, qg, kg, hg,
                pool_w, pool_scale, w_out, mstack, lmask):
    d = h_in.shape[1]
    n_steps = cos.shape[0] // STEP

    def rows(width, lag, skip_front=False):
        off = lag + (1 if skip_front else 0)
        hi = n_steps - 1 - (1 if skip_front else 0)
        return pl.BlockSpec((STEP, width), lambda t: (jnp.clip(t - off, 0, hi), 0))

    def tile_rows(width, tile_off):
        return pl.BlockSpec((TILE, width), lambda t: (jnp.clip(2 * t + tile_off, 0, 2 * n_steps - 1), 0))

    def whole(shape, single=False):
        idx = lambda t: (0,) * len(shape)
        if single:
            return pl.BlockSpec(shape, idx, pipeline_mode=pl.Buffered(1))
        return pl.BlockSpec(shape, idx)

    in_specs = [pl.BlockSpec(memory_space=pltpu.SMEM)]
    args = [sinks]
    for lag in (0, 1):
        in_specs.append(rows(d, lag, skip_front=first))
        args.append(h_in)
        if first:
            in_specs.append(whole(meta.shape))
            args.append(meta)
    in_specs += [
        tile_rows(LANES, -1), tile_rows(LANES, -1),
        tile_rows(LANES, 0), tile_rows(LANES, 0),
        whole(norm_g.shape),
        pl.BlockSpec(memory_space=pl.ANY),
        whole(lb_logits.shape),
        whole(qg.shape), whole(kg.shape), whole(hg.shape),
        whole(pool_w.shape), whole(pool_scale.shape),
        pl.BlockSpec(memory_space=pl.ANY),
        whole(mstack.shape, single=True), whole(lmask.shape, single=True),
    ]
    args += [cos, sin, cos, sin, norm_g, w_in, lb_logits, qg, kg, hg, pool_w, pool_scale, w_out, mstack, lmask]
    out_rows = (n_steps - 1) * STEP if last else n_steps * STEP
    assert w_in.shape[1] % (2 * TILE) == 0 and w_out.shape[1] % (2 * TILE) == 0
    assert w_out.shape[2] <= IN_COLS
    scratch = [
        pltpu.VMEM(w_in.shape[1:], BF16),
        pltpu.VMEM(w_out.shape[1:], BF16),
        pltpu.SemaphoreType.DMA((2,)),
        pltpu.VMEM((STEP, d), BF16),
        pltpu.VMEM((TILE, IN_COLS), F32),
        pltpu.VMEM((TILE, IN_COLS), F32),
        pltpu.VMEM((TILE, MIX_WIDTH), BF16),
        pltpu.VMEM((TILE, MIX_WIDTH), BF16),
        pltpu.VMEM((EXPO_BLOCKS * TILE, A_WIDTH), F32),
        pltpu.VMEM((HA, DK_A, DK_A), F32),
        pltpu.VMEM((KBUF_ROWS, LANES), BF16),
        pltpu.VMEM((KBUF_ROWS, LANES), BF16),
        pltpu.VMEM((KBUF_ROWS, LANES), BF16),
        pltpu.VMEM((KBUF_ROWS, LANES), BF16),
        pltpu.VMEM((max(POOL_WINDOWS) + TILE, C_WIDTH), F32),
    ]
    return pl.pallas_call(
        functools.partial(_layer_kernel, layer, first),
        grid=(n_steps + 1,),
        in_specs=in_specs,
        out_specs=rows(d, 1, skip_front=last),
        out_shape=jax.ShapeDtypeStruct((out_rows, d), F32),
        scratch_shapes=scratch,
        compiler_params=pltpu.CompilerParams(
            dimension_semantics=("arbitrary",), vmem_limit_bytes=VMEM_LIMIT),
        name=f"hybrid_layer{layer}",
    )(*args)


def kernel(x, meta_tokens, lb_logits, norm_g, w_in, q_norm_g, k_norm_g, attn_sinks, hgrn_norm_g,
           pool_w, pool_scale, w_out):
    b, seq, d = x.shape
    depth = w_in.shape[0]
    assert b == 1 and seq % STEP == 0
    assert w_in.shape[2] == IN_COLS and w_out.shape[1] == MIX_WIDTH
    assert meta_tokens.shape[0] == N_META

    p = STEP + seq
    pos = (jnp.arange(p) - (STEP - N_META)).astype(F32)
    half = DH_B // 2
    inv = jnp.power(ROPE_THETA, -jnp.arange(half, dtype=F32) * 2.0 / DH_B)
    ang = pos[:, None] * inv[None, :]
    cos = jnp.tile(jnp.cos(ang), (1, LANES // half))
    sin = jnp.tile(jnp.concatenate([-jnp.sin(ang), jnp.sin(ang)], axis=1), (1, LANES // DH_B))

    mstack_np, lmask_np = _decay_constants()
    mstack = jnp.asarray(mstack_np, BF16)
    lmask = jnp.asarray(lmask_np, F32)

    h = x[0]
    for l in range(depth):
        h = _layer_call(
            l, l == 0, l == depth - 1, h, meta_tokens.astype(F32), cos, sin, attn_sinks[l],
            norm_g[l][None, :], w_in, lb_logits,
            jnp.tile(q_norm_g[l], LANES // DH_B)[None, :], jnp.tile(k_norm_g[l], LANES // DH_B)[None, :],
            hgrn_norm_g[l][None, :], pool_w[l].astype(BF16), pool_scale[l][None, :],
            w_out, mstack, lmask)
    return h[None]
```

```python
import functools

import numpy as np
import jax
import jax.numpy as jnp
from jax import lax
from jax.experimental import pallas as pl
from jax.experimental.pallas import tpu as pltpu

F32 = jnp.float32
BF16 = jnp.bfloat16

N_META = 16
TILE = 128
PAD_FRONT = TILE - N_META
STEP = 2 * TILE
RMS_EPS = 1e-6
NEG_INF = -1e30
LOG_FLOOR = 1e-30
ROPE_THETA = 10000.0
LOG2E = 1.4426950408889634

HA, DK_A = 4, 128
HGRN_GROUP = 2
A_WIDTH = HA * DK_A
HB, KVH_B, DH_B = 16, 2, 64
B_WIDTH = HB * DH_B
KV_WIDTH = KVH_B * DH_B
POOL_WINDOWS = (2, 4, 8, 16)
CG_C = 128
C_WIDTH = len(POOL_WINDOWS) * CG_C
MIX_WIDTH = A_WIDTH + B_WIDTH + C_WIDTH

OFF_QA = 0
OFF_FA = OFF_QA + A_WIDTH
OFF_IA = OFF_FA + A_WIDTH
OFF_GA = OFF_IA + A_WIDTH
OFF_QB = OFF_GA + A_WIDTH
OFF_KB = OFF_QB + B_WIDTH
OFF_VB = OFF_KB + KV_WIDTH
OFF_GB = OFF_VB + KV_WIDTH
OFF_UC = OFF_GB + B_WIDTH
OFF_GC = OFF_UC + C_WIDTH
IN_COLS = OFF_GC + C_WIDTH
MM_CHUNK = 256

LANES = 128
QB = 64
KWIN = 256
KB_META0 = KWIN - N_META - TILE - QB
KB_PREV = KB_META0 + N_META
KB_CUR = KB_PREV + TILE
KB_META1 = KB_CUR + TILE
KBUF_ROWS = KB_PREV + QB + KWIN
assert KB_META0 >= 16 and KBUF_ROWS - KB_META1 - N_META >= 16
LEVELS = (128, 64, 32, 16, 8, 4, 2)
SUBLANES = 8
EXPO_BLOCKS = 2
MIX_TICKS = 1 + HA + 2 + 2 * (TILE // QB) + len(POOL_WINDOWS)
VMEM_LIMIT = 60 * 1024 * 1024


def _decay_constants():
    r = np.arange(TILE)
    masks = []
    for blk in LEVELS:
        half = blk // 2
        pos = r % blk
        same = (r[:, None] // blk) == (r[None, :] // blk)
        masks.append((same & (pos[:, None] >= half) & (pos[None, :] < half)).astype(np.float32))
    tril = np.tril(np.ones((TILE, TILE), np.float32))
    return np.concatenate([tril, tril], axis=1), np.stack(masks, axis=0)


def _dot(a, b):
    return jnp.dot(a, b, preferred_element_type=F32)


def _dot_nt(a, b):
    return lax.dot_general(a, b, (((1,), (1,)), ((), ())), preferred_element_type=F32)


def _dot_tn(a, b):
    return lax.dot_general(a, b, (((0,), (0,)), ((), ())), preferred_element_type=F32)


def _sigmoid(x):
    return 0.5 * jnp.tanh(0.5 * x) + 0.5


def _silu(x):
    t = 0.5 * x
    return t * jnp.tanh(t) + t


def _mix_tile(layer, blk, sinks_ref, cos_ref, sin_ref, lbl_ref, qg_ref, kg_ref, hg_ref,
              pw_ref, ps_ref, mstack_ref, lmask_ref, proj_ref, mixed_ref, expo_ref, st_ref,
              kbuf_ref, ksw_ref, vbuf_ref, vsw_ref, uext_ref, tick):
    rloc = lax.broadcasted_iota(jnp.int32, (TILE, 1), 0)
    row = blk * TILE + rloc
    valid = (row >= PAD_FRONT).astype(F32)

    rows = [lbl_ref[j:j + 1, :] for j in range(lbl_ref.shape[0])]
    mx = functools.reduce(jnp.maximum, rows)
    es = [jnp.exp(x - mx) for x in rows]
    lb = sum(es[1:layer + 1], jnp.zeros_like(mx)) / sum(es)

    z = proj_ref[:, OFF_FA:OFF_FA + A_WIDTH]
    sig = _sigmoid(z)
    f = lb + (1.0 - lb) * sig
    logf = jnp.log2(jnp.maximum(f, LOG_FLOOR)) * valid
    kk = (1.0 - lb) * (1.0 - sig) * valid
    proj_ref[:, OFF_QA:OFF_QA + A_WIDTH] = _silu(proj_ref[:, OFF_QA:OFF_QA + A_WIDTH])
    proj_ref[:, OFF_FA:OFF_FA + A_WIDTH] = kk
    lf_hi = logf.astype(BF16)
    lf_lo = (logf - lf_hi.astype(F32)).astype(BF16)
    expo_ref[0:TILE, :] = _dot(mstack_ref[...], jnp.concatenate([lf_hi, lf_lo], axis=0))
    expo_ref[TILE:2 * TILE, :] = jnp.where(valid > 0.0, jnp.maximum(f, LOG_FLOOR), 1.0)
    tick()

    def level_factors(bsz, g, f_row):
        half = bsz // 2
        if bsz == 2:
            return jnp.where((rloc & 1) == 1, f_row, 1.0)
        if bsz <= SUBLANES:
            sub = lax.broadcasted_iota(jnp.int32, (SUBLANES, 1), 0)
            vregs = []
            for r0 in range(0, TILE, SUBLANES):
                gv = g[r0:r0 + SUBLANES]
                piv = None
                for b0 in range(0, SUBLANES, bsz):
                    row = jnp.broadcast_to(gv[b0 + half - 1:b0 + half, :], gv.shape)
                    piv = row if piv is None else jnp.where(sub >= b0, row, piv)
                vregs.append(gv - piv)
            d = jnp.concatenate(vregs, axis=0)
            return jnp.exp2(jnp.where((rloc & (bsz - 1)) >= half, d, -d))
        parts = []
        for b0 in range(0, TILE, bsz):
            piv = jnp.broadcast_to(g[b0 + half - 1:b0 + half, :], (half, g.shape[1]))
            parts += [piv - g[b0:b0 + half], g[b0 + half:b0 + bsz] - piv]
        return jnp.exp2(jnp.concatenate(parts, axis=0))

    def hgrn_heads(hds):
        cs = {hd: slice(hd * DK_A, (hd + 1) * DK_A) for hd in hds}
        qf_h = {hd: proj_ref[:, OFF_QA + hd * DK_A:OFF_QA + (hd + 1) * DK_A] for hd in hds}
        kk_h = {hd: proj_ref[:, OFF_FA + hd * DK_A:OFF_FA + (hd + 1) * DK_A] for hd in hds}
        diag = {hd: jnp.sum(qf_h[hd] * kk_h[hd], axis=-1, keepdims=True) for hd in hds}
        attn = {}
        g_inc = {hd: expo_ref[0:TILE, cs[hd]] for hd in hds}
        f_row = {hd: expo_ref[TILE:2 * TILE, cs[hd]] for hd in hds}
        for li, bsz in enumerate(LEVELS):
            upper = (rloc & (bsz - 1)) >= (bsz // 2)
            for hd in hds:
                xb = (jnp.where(upper, qf_h[hd], kk_h[hd])
                      * level_factors(bsz, g_inc[hd], f_row[hd])).astype(BF16)
                term = _dot_nt(xb, xb) * lmask_ref[li]
                attn[hd] = term if li == 0 else attn[hd] + term
            if li == len(LEVELS) // 2:
                tick()
        v_f32 = {hd: proj_ref[:, OFF_IA + hd * DK_A:OFF_IA + (hd + 1) * DK_A] for hd in hds}
        v_h = {hd: v_f32[hd].astype(BF16) for hd in hds}
        st = {hd: st_ref[hd] for hd in hds}
        o = {hd: _dot(attn[hd].astype(BF16), v_h[hd]) + diag[hd] * v_f32[hd] for hd in hds}
        for hd in hds:
            q_dec = (qf_h[hd] * jnp.exp2(g_inc[hd])).astype(BF16)
            o[hd] = o[hd] + _dot_nt(q_dec, st[hd].astype(BF16))
        for hd in hds:
            g_tot = g_inc[hd][TILE - 1:TILE]
            k_dec = (kk_h[hd] * jnp.exp2(g_tot - g_inc[hd])).astype(BF16)
            st_ref[hd] = jnp.exp2(g_tot) * st[hd] + _dot_tn(v_h[hd], k_dec)
        for _ in range(len(hds) - 1):
            tick()
        for hd in hds:
            ya = o[hd] * lax.rsqrt(jnp.mean(o[hd] * o[hd], axis=-1, keepdims=True) + RMS_EPS) * hg_ref[...]
            ga = proj_ref[:, OFF_GA + hd * DK_A:OFF_GA + (hd + 1) * DK_A]
            mixed_ref[:, hd * DK_A:(hd + 1) * DK_A] = (ya * _silu(ga)).astype(BF16)

    for hd0 in range(0, HA, HGRN_GROUP):
        hgrn_heads(tuple(range(hd0, hd0 + HGRN_GROUP)))

    lane = lax.broadcasted_iota(jnp.int32, (1, LANES), 1)
    lo_half = lane < DH_B
    first = (lane & (DH_B - 1)) < DH_B // 2
    cos = cos_ref[...]
    sin = sin_ref[...]

    def norm_rope(x, g):
        sq = x * x
        s0 = jnp.sum(jnp.where(lo_half, sq, 0.0), axis=-1, keepdims=True)
        s1 = jnp.sum(jnp.where(lo_half, 0.0, sq), axis=-1, keepdims=True)
        msq = jnp.where(lo_half, s0, s1) * (1.0 / DH_B)
        y = x * lax.rsqrt(msq + RMS_EPS) * g
        rot = jnp.where(first, pltpu.roll(y, LANES - DH_B // 2, axis=1),
                        pltpu.roll(y, DH_B // 2, axis=1))
        return y * cos + rot * sin

    kc = norm_rope(proj_ref[:, OFF_KB:OFF_KB + KV_WIDTH], kg_ref[...])
    vc = proj_ref[:, OFF_VB:OFF_VB + KV_WIDTH]
    is_meta_tile = blk == 0
    for buf, val in ((kbuf_ref, kc), (ksw_ref, pltpu.roll(kc, DH_B, axis=1)),
                     (vbuf_ref, vc), (vsw_ref, pltpu.roll(vc, DH_B, axis=1))):
        val = val.astype(BF16)
        buf[KB_CUR:KB_CUR + TILE, :] = val
        for m0 in (KB_META0, KB_META1):
            buf[m0:m0 + N_META, :] = jnp.where(is_meta_tile, val[PAD_FRONT:], buf[m0:m0 + N_META, :])
    tick()

    scale = DH_B ** -0.5 * LOG2E
    qcols = []
    for j in range(HB // 2):
        qj = norm_rope(proj_ref[:, OFF_QB + j * LANES:OFF_QB + (j + 1) * LANES], qg_ref[...]) * scale
        qcols.append((jnp.where(lo_half, qj, 0.0).astype(BF16), jnp.where(lo_half, 0.0, qj).astype(BF16)))
    tick()

    grp = HB // KVH_B
    heads_same = [hh for hh in range(HB) if (hh % 2) == (hh // grp)]
    heads_swap = [hh for hh in range(HB) if (hh % 2) != (hh // grp)]
    col = lax.broadcasted_iota(jnp.int32, (1, KWIN), 1)
    big = jnp.int32(1 << 30)
    windows = ((0, 0), (KBUF_ROWS - KWIN, KWIN - 1))

    def attn_sub_block(sb):
        w0, sink_col = windows[sb]
        q0 = sb * QB
        brow = w0 + col
        in_meta0 = (brow >= KB_META0) & (brow < KB_PREV)
        in_meta1 = (brow >= KB_META1) & (brow < KB_META1 + N_META)
        in_band = (brow >= KB_PREV) & (brow < KB_META1)
        kj = jnp.where(in_meta0, PAD_FRONT + brow - KB_META0,
                       jnp.where(in_meta1, PAD_FRONT + brow - KB_META1,
                                 jnp.where(in_band, (blk - 1) * TILE + brow - KB_PREV, big)))
        qi = blk * TILE + q0 + lax.broadcasted_iota(jnp.int32, (QB, 1), 0)
        allowed = (kj <= qi) & (in_meta0 | in_meta1 | ((kj > qi - TILE) & (kj >= TILE)))

        groups = ((heads_same, kbuf_ref, vbuf_ref), (heads_swap, ksw_ref, vsw_ref))
        s_all = []
        for heads, k_ref, _ in groups:
            lhs = jnp.concatenate([qcols[hh // 2][hh % 2][q0:q0 + QB] for hh in heads], axis=0)
            s_all.append(_dot_nt(lhs, k_ref[w0:w0 + KWIN, :]))
        ps, inv_den = [[] for _ in groups], [[] for _ in groups]
        for n in range(len(heads_same)):
            for g, (heads, _, _) in enumerate(groups):
                fill = jnp.where(col == sink_col, sinks_ref[heads[n]] * LOG2E, NEG_INF)
                s = jnp.where(allowed, s_all[g][n * QB:(n + 1) * QB], fill)
                pr = jnp.exp2(s - jnp.max(s, axis=-1, keepdims=True))
                inv_den[g].append(1.0 / jnp.sum(pr, axis=-1, keepdims=True))
                ps[g].append(pr.astype(BF16))
            if n == len(heads_same) // 2 - 1:
                tick()
        outs = {}
        for g, (heads, _, v_ref) in enumerate(groups):
            o_all = _dot(jnp.concatenate(ps[g], axis=0), v_ref[w0:w0 + KWIN, :])
            for n, hh in enumerate(heads):
                outs[hh] = o_all[n * QB:(n + 1) * QB] * inv_den[g][n]
        tick()
        for j in range(HB // 2):
            yb = jnp.where(lo_half, outs[2 * j], outs[2 * j + 1])
            gb = proj_ref[q0:q0 + QB, OFF_GB + j * LANES:OFF_GB + (j + 1) * LANES]
            mixed_ref[q0:q0 + QB, A_WIDTH + j * LANES:A_WIDTH + (j + 1) * LANES] = (
                yb * _silu(gb)).astype(BF16)

    for sb in range(len(windows)):
        attn_sub_block(sb)

    for buf in (kbuf_ref, ksw_ref, vbuf_ref, vsw_ref):
        buf[KB_PREV:KB_CUR, :] = buf[KB_CUR:KB_CUR + TILE, :]

    wmax = max(POOL_WINDOWS)
    ug = proj_ref[:, OFF_UC:OFF_UC + C_WIDTH] * valid
    uext_ref[wmax:wmax + TILE, :] = ug
    n_valid = jnp.maximum(row - (PAD_FRONT - 1), 0)
    for g, w in enumerate(POOL_WINDOWS):
        cs = slice(g * CG_C, (g + 1) * CG_C)
        acc = uext_ref[wmax:wmax + TILE, cs]
        for s in range(1, w):
            acc = acc + uext_ref[wmax - s:wmax - s + TILE, cs]
        cnt = (n_valid - jnp.maximum(row - w - (PAD_FRONT - 1), 0)).astype(F32)
        pooled = (acc / jnp.maximum(cnt, 1.0) - ug[:, cs]) * valid
        y = _dot(pooled.astype(BF16), pw_ref[g]) * ps_ref[:, cs]
        gc = proj_ref[:, OFF_GC + g * CG_C:OFF_GC + (g + 1) * CG_C]
        mixed_ref[:, A_WIDTH + B_WIDTH + g * CG_C:A_WIDTH + B_WIDTH + (g + 1) * CG_C] = (
            y * _silu(gc)).astype(BF16)
        tick()
    uext_ref[0:wmax, :] = ug[TILE - wmax:]


def _layer_kernel(layer, first, sinks_ref, *refs):
    n_h = 4 if first else 2
    h_refs, refs = refs[:n_h], refs[n_h:]
    (cosb_ref, sinb_ref, cosa_ref, sina_ref, ng_ref, win_hbm, lbl_ref, qg_ref, kg_ref, hg_ref, pw_ref,
     ps_ref, wout_hbm, mstack_ref, lmask_ref, out_ref, win_ref, wout_ref, wsem, xn_ref, proja_ref,
     projb_ref, mixeda_ref, mixedb_ref, expo_ref, st_ref, kbuf_ref, ksw_ref, vbuf_ref, vsw_ref,
     uext_ref) = refs
    t = pl.program_id(0)

    @pl.when(t == 0)
    def _load_weights_and_zero_state():
        stage = (proja_ref, projb_ref)
        for w_hbm, w_vmem in ((win_hbm, win_ref), (wout_hbm, wout_ref)):
            width = w_vmem.shape[1]
            n_chunks = w_vmem.shape[0] // TILE

            def chunk_copy(k, slot, w_hbm=w_hbm, width=width):
                return pltpu.make_async_copy(w_hbm.at[layer, pl.ds(k * TILE, TILE), :],
                                             stage[slot].at[:, 0:width], wsem.at[slot])

            chunk_copy(0, 0).start()
            chunk_copy(1, 1).start()

            def pair(i, carry, w_vmem=w_vmem, width=width, n_chunks=n_chunks, chunk_copy=chunk_copy):
                for slot in (0, 1):
                    k = 2 * i + slot
                    chunk_copy(k, slot).wait()
                    w_vmem[pl.ds(pl.multiple_of(k * TILE, TILE), TILE), :] = stage[slot][:, 0:width].astype(BF16)

                    @pl.when(k + 2 < n_chunks)
                    def _next():
                        chunk_copy(k + 2, slot).start()
                return carry

            lax.fori_loop(0, n_chunks // 2, pair, 0)

        for ref in (proja_ref, projb_ref, mixeda_ref, mixedb_ref, st_ref, kbuf_ref, ksw_ref,
                    vbuf_ref, vsw_ref, uext_ref):
            ref[...] = jnp.zeros_like(ref)

    def step_rows(step_idx, rws, cols, x_ref, meta_ref=None):
        x = x_ref[rws, cols]
        if not first:
            return x
        if rws.stop is not None and rws.stop < STEP:
            return jnp.where(step_idx == 0, 0.0, x)
        n_rows = x.shape[0]
        front = jnp.concatenate([jnp.zeros((n_rows - N_META, x.shape[1]), F32), meta_ref[:, cols]], axis=0)
        return jnp.where(step_idx == 0, front, x)

    d_model = out_ref.shape[1]
    last_step = pl.num_programs(0) - 1
    rows_a, rows_b = slice(0, TILE), slice(TILE, STEP)

    def norm_rows():
        h = step_rows(t, slice(None), slice(None), *h_refs[:n_h // 2])
        ms = jnp.mean(h * h, axis=-1, keepdims=True)
        xn_ref[...] = (h * lax.rsqrt(ms + RMS_EPS) * ng_ref[...]).astype(BF16)

    def out_chunk(rws, mixed_src, c0):
        cols = slice(c0, c0 + MM_CHUNK)
        h_res = step_rows(t - 1, rws, cols, *h_refs[n_h // 2:])
        out_ref[rws, cols] = h_res + _dot(mixed_src[...], wout_ref[:, cols])

    def in_chunk(rws, proj_dst, c0):
        cols = slice(c0, c0 + MM_CHUNK)
        proj_dst[:, cols] = _dot(xn_ref[rws, :], win_ref[:, cols])

    def out_chunks(rws, mixed_src):
        return [functools.partial(out_chunk, rws, mixed_src, c0) for c0 in range(0, d_model, MM_CHUNK)]

    def in_chunks(rws, proj_dst):
        return [functools.partial(in_chunk, rws, proj_dst, c0) for c0 in range(0, IN_COLS, MM_CHUNK)]

    def half_step(work, blk_mix, trig, proj_mix, mixed_mix):
        progress = {"ticks": 0, "done": 0}

        def tick():
            progress["ticks"] += 1
            target = min(len(work), -(-len(work) * progress["ticks"] // (MIX_TICKS + 1)))
            while progress["done"] < target:
                work[progress["done"]]()
                progress["done"] += 1

        tick()
        _mix_tile(layer, blk_mix, sinks_ref, trig[0], trig[1], lbl_ref, qg_ref, kg_ref, hg_ref,
                  pw_ref, ps_ref, mstack_ref, lmask_ref, proj_mix, mixed_mix, expo_ref, st_ref,
                  kbuf_ref, ksw_ref, vbuf_ref, vsw_ref, uext_ref, tick)
        assert progress["done"] == len(work), "MIX_TICKS must match the tick() calls in _mix_tile"

    trig_a, trig_b = (cosa_ref, sina_ref), (cosb_ref, sinb_ref)

    @pl.when(t == 0)
    def _first_step():
        norm_rows()
        for chunk in in_chunks(rows_b, projb_ref):
            chunk()

    @pl.when((t > 0) & (t < last_step))
    def _main_step():
        norm_rows()
        half_step(out_chunks(rows_a, mixeda_ref) + in_chunks(rows_a, proja_ref), 2 * t - 2, trig_b,
                  projb_ref, mixedb_ref)
        half_step(out_chunks(rows_b, mixedb_ref) + in_chunks(rows_b, projb_ref), 2 * t - 1, trig_a,
                  proja_ref, mixeda_ref)

    @pl.when(t == last_step)
    def _drain_step():
        half_step(out_chunks(rows_a, mixeda_ref), 2 * t - 2, trig_b, projb_ref, mixedb_ref)
        for chunk in out_chunks(rows_b, mixedb_ref):
            chunk()


def _layer_call(layer, first, last, h_in, meta, cos, sin, sinks, norm_g, w_in, lb_logits, qg, kg, hg,
                pool_w, pool_scale, w_out, mstack, lmask):
    d = h_in.shape[1]
    n_steps = cos.shape[0] // STEP

    def rows(width, lag, skip_front=False):
        off = lag + (1 if skip_front else 0)
        hi = n_steps - 1 - (1 if skip_front else 0)
        return pl.BlockSpec((STEP, width), lambda t: (jnp.clip(t - off, 0, hi), 0))

    def tile_rows(width, tile_off):
        return pl.BlockSpec((TILE, width), lambda t: (jnp.clip(2 * t + tile_off, 0, 2 * n_steps - 1), 0))

    def whole(shape, single=False):
        idx = lambda t: (0,) * len(shape)
        if single:
            return pl.BlockSpec(shape, idx, pipeline_mode=pl.Buffered(1))
        return pl.BlockSpec(shape, idx)

    in_specs = [pl.BlockSpec(memory_space=pltpu.SMEM)]
    args = [sinks]
    for lag in (0, 1):
        in_specs.append(rows(d, lag, skip_front=first))
        args.append(h_in)
        if first:
            in_specs.append(whole(meta.shape))
            args.append(meta)
    in_specs += [
        tile_rows(LANES, -1), tile_rows(LANES, -1),
        tile_rows(LANES, 0), tile_rows(LANES, 0),
        whole(norm_g.shape),
        pl.BlockSpec(memory_space=pl.ANY),
        whole(lb_logits.shape),
        whole(qg.shape), whole(kg.shape), whole(hg.shape),
        whole(pool_w.shape), whole(pool_scale.shape),
        pl.BlockSpec(memory_space=pl.ANY),
        whole(mstack.shape, single=True), whole(lmask.shape, single=True),
    ]
    args += [cos, sin, cos, sin, norm_g, w_in, lb_logits, qg, kg, hg, pool_w, pool_scale, w_out, mstack, lmask]
    out_rows = (n_steps - 1) * STEP if last else n_steps * STEP
    assert w_in.shape[1] % (2 * TILE) == 0 and w_out.shape[1] % (2 * TILE) == 0
    assert w_out.shape[2] <= IN_COLS
    scratch = [
        pltpu.VMEM(w_in.shape[1:], BF16),
        pltpu.VMEM(w_out.shape[1:], BF16),
        pltpu.SemaphoreType.DMA((2,)),
        pltpu.VMEM((STEP, d), BF16),
        pltpu.VMEM((TILE, IN_COLS), F32),
        pltpu.VMEM((TILE, IN_COLS), F32),
        pltpu.VMEM((TILE, MIX_WIDTH), BF16),
        pltpu.VMEM((TILE, MIX_WIDTH), BF16),
        pltpu.VMEM((EXPO_BLOCKS * TILE, A_WIDTH), F32),
        pltpu.VMEM((HA, DK_A, DK_A), F32),
        pltpu.VMEM((KBUF_ROWS, LANES), BF16),
        pltpu.VMEM((KBUF_ROWS, LANES), BF16),
        pltpu.VMEM((KBUF_ROWS, LANES), BF16),
        pltpu.VMEM((KBUF_ROWS, LANES), BF16),
        pltpu.VMEM((max(POOL_WINDOWS) + TILE, C_WIDTH), F32),
    ]
    return pl.pallas_call(
        functools.partial(_layer_kernel, layer, first),
        grid=(n_steps + 1,),
        in_specs=in_specs,
        out_specs=rows(d, 1, skip_front=last),
        out_shape=jax.ShapeDtypeStruct((out_rows, d), F32),
        scratch_shapes=scratch,
        compiler_params=pltpu.CompilerParams(
            dimension_semantics=("arbitrary",), vmem_limit_bytes=VMEM_LIMIT),
        name=f"hybrid_layer{layer}",
    )(*args)


def kernel(x, meta_tokens, lb_logits, norm_g, w_in, q_norm_g, k_norm_g, attn_sinks, hgrn_norm_g,
           pool_w, pool_scale, w_out):
    b, seq, d = x.shape
    depth = w_in.shape[0]
    assert b == 1 and seq % STEP == 0
    assert w_in.shape[2] == IN_COLS and w_out.shape[1] == MIX_WIDTH
    assert meta_tokens.shape[0] == N_META

    p = STEP + seq
    pos = (jnp.arange(p) - (STEP - N_META)).astype(F32)
    half = DH_B // 2
    inv = jnp.power(ROPE_THETA, -jnp.arange(half, dtype=F32) * 2.0 / DH_B)
    ang = pos[:, None] * inv[None, :]
    cos = jnp.tile(jnp.cos(ang), (1, LANES // half))
    sin = jnp.tile(jnp.concatenate([-jnp.sin(ang), jnp.sin(ang)], axis=1), (1, LANES // DH_B))

    mstack_np, lmask_np = _decay_constants()
    mstack = jnp.asarray(mstack_np, BF16)
    lmask = jnp.asarray(lmask_np, F32)

    h = x[0]
    for l in range(depth):
        h = _layer_call(
            l, l == 0, l == depth - 1, h, meta_tokens.astype(F32), cos, sin, attn_sinks[l],
            norm_g[l][None, :], w_in, lb_logits,
            jnp.tile(q_norm_g[l], LANES // DH_B)[None, :], jnp.tile(k_norm_g[l], LANES // DH_B)[None, :],
            hgrn_norm_g[l][None, :], pool_w[l].astype(BF16), pool_scale[l][None, :],
            w_out, mstack, lmask)
    return h[None]
```

```python
import functools

import numpy as np
import jax
import jax.numpy as jnp
from jax import lax
from jax.experimental import pallas as pl
from jax.experimental.pallas import tpu as pltpu

F32 = jnp.float32
BF16 = jnp.bfloat16

N_META = 16
TILE = 128
PAD_FRONT = TILE - N_META
STEP = 2 * TILE
RMS_EPS = 1e-6
NEG_INF = -1e30
LOG_FLOOR = 1e-30
ROPE_THETA = 10000.0
LOG2E = 1.4426950408889634

HA, DK_A = 4, 128
HGRN_GROUP = 2
A_WIDTH = HA * DK_A
HB, KVH_B, DH_B = 16, 2, 64
B_WIDTH = HB * DH_B
KV_WIDTH = KVH_B * DH_B
POOL_WINDOWS = (2, 4, 8, 16)
CG_C = 128
C_WIDTH = len(POOL_WINDOWS) * CG_C
MIX_WIDTH = A_WIDTH + B_WIDTH + C_WIDTH

OFF_QA = 0
OFF_FA = OFF_QA + A_WIDTH
OFF_IA = OFF_FA + A_WIDTH
OFF_GA = OFF_IA + A_WIDTH
OFF_QB = OFF_GA + A_WIDTH
OFF_KB = OFF_QB + B_WIDTH
OFF_VB = OFF_KB + KV_WIDTH
OFF_GB = OFF_VB + KV_WIDTH
OFF_UC = OFF_GB + B_WIDTH
OFF_GC = OFF_UC + C_WIDTH
IN_COLS = OFF_GC + C_WIDTH
MM_CHUNK = 256

LANES = 128
QB = 64
KWIN = 256
KB_META0 = KWIN - N_META - TILE - QB
KB_PREV = KB_META0 + N_META
KB_CUR = KB_PREV + TILE
KB_META1 = KB_CUR + TILE
KBUF_ROWS = KB_PREV + QB + KWIN
assert KB_META0 >= 16 and KBUF_ROWS - KB_META1 - N_META >= 16
LEVELS = (128, 64, 32, 16, 8, 4, 2)
SUBLANES = 8
EXPO_BLOCKS = 2
MIX_TICKS = 1 + HA + 2 + 2 * (TILE // QB) + len(POOL_WINDOWS)
V7X_VMEM_BYTES = 64 * 1024 * 1024
VMEM_LIMIT = V7X_VMEM_BYTES - 4 * 1024 * 1024


def _decay_constants():
    r = np.arange(TILE)
    masks = []
    for blk in LEVELS:
        half = blk // 2
        pos = r % blk
        same = (r[:, None] // blk) == (r[None, :] // blk)
        masks.append((same & (pos[:, None] >= half) & (pos[None, :] < half)).astype(np.float32))
    tril = np.tril(np.ones((TILE, TILE), np.float32))
    return np.concatenate([tril, tril], axis=1), np.stack(masks, axis=0)


def _dot(a, b):
    return jnp.dot(a, b, preferred_element_type=F32)


def _dot_nt(a, b):
    return lax.dot_general(a, b, (((1,), (1,)), ((), ())), preferred_element_type=F32)


def _dot_tn(a, b):
    return lax.dot_general(a, b, (((0,), (0,)), ((), ())), preferred_element_type=F32)


def _sigmoid(x):
    return 0.5 * jnp.tanh(0.5 * x) + 0.5


def _silu(x):
    t = 0.5 * x
    return t * jnp.tanh(t) + t


def _mix_tile(layer, blk, sinks_ref, cos_ref, sin_ref, lbl_ref, qg_ref, kg_ref, hg_ref,
              pw_ref, ps_ref, mstack_ref, lmask_ref, proj_ref, mixed_ref, expo_ref, st_ref,
              kbuf_ref, ksw_ref, vbuf_ref, vsw_ref, uext_ref, tick):
    rloc = lax.broadcasted_iota(jnp.int32, (TILE, 1), 0)
    row = blk * TILE + rloc
    valid = (row >= PAD_FRONT).astype(F32)

    rows = [lbl_ref[j:j + 1, :] for j in range(lbl_ref.shape[0])]
    mx = functools.reduce(jnp.maximum, rows)
    es = [jnp.exp(x - mx) for x in rows]
    lb = sum(es[1:layer + 1], jnp.zeros_like(mx)) / sum(es)

    z = proj_ref[:, OFF_FA:OFF_FA + A_WIDTH]
    sig = _sigmoid(z)
    f = lb + (1.0 - lb) * sig
    logf = jnp.log2(jnp.maximum(f, LOG_FLOOR)) * valid
    kk = (1.0 - lb) * (1.0 - sig) * valid
    proj_ref[:, OFF_QA:OFF_QA + A_WIDTH] = _silu(proj_ref[:, OFF_QA:OFF_QA + A_WIDTH])
    proj_ref[:, OFF_FA:OFF_FA + A_WIDTH] = kk
    lf_hi = logf.astype(BF16)
    lf_lo = (logf - lf_hi.astype(F32)).astype(BF16)
    expo_ref[0:TILE, :] = _dot(mstack_ref[...], jnp.concatenate([lf_hi, lf_lo], axis=0))
    expo_ref[TILE:2 * TILE, :] = jnp.where(valid > 0.0, jnp.maximum(f, LOG_FLOOR), 1.0)
    tick()

    def level_factors(bsz, g, f_row):
        half = bsz // 2
        if bsz == 2:
            return jnp.where((rloc & 1) == 1, f_row, 1.0)
        if bsz <= SUBLANES:
            sub = lax.broadcasted_iota(jnp.int32, (SUBLANES, 1), 0)
            vregs = []
            for r0 in range(0, TILE, SUBLANES):
                gv = g[r0:r0 + SUBLANES]
                piv = None
                for b0 in range(0, SUBLANES, bsz):
                    row = jnp.broadcast_to(gv[b0 + half - 1:b0 + half, :], gv.shape)
                    piv = row if piv is None else jnp.where(sub >= b0, row, piv)
                vregs.append(gv - piv)
            d = jnp.concatenate(vregs, axis=0)
            return jnp.exp2(jnp.where((rloc & (bsz - 1)) >= half, d, -d))
        parts = []
        for b0 in range(0, TILE, bsz):
            piv = jnp.broadcast_to(g[b0 + half - 1:b0 + half, :], (half, g.shape[1]))
            parts += [piv - g[b0:b0 + half], g[b0 + half:b0 + bsz] - piv]
        return jnp.exp2(jnp.concatenate(parts, axis=0))

    def hgrn_heads(hds):
        cs = {hd: slice(hd * DK_A, (hd + 1) * DK_A) for hd in hds}
        qf_h = {hd: proj_ref[:, OFF_QA + hd * DK_A:OFF_QA + (hd + 1) * DK_A] for hd in hds}
        kk_h = {hd: proj_ref[:, OFF_FA + hd * DK_A:OFF_FA + (hd + 1) * DK_A] for hd in hds}
        diag = {hd: jnp.sum(qf_h[hd] * kk_h[hd], axis=-1, keepdims=True) for hd in hds}
        attn = {}
        g_inc = {hd: expo_ref[0:TILE, cs[hd]] for hd in hds}
        f_row = {hd: expo_ref[TILE:2 * TILE, cs[hd]] for hd in hds}
        for li, bsz in enumerate(LEVELS):
            upper = (rloc & (bsz - 1)) >= (bsz // 2)
            for hd in hds:
                xb = (jnp.where(upper, qf_h[hd], kk_h[hd])
                      * level_factors(bsz, g_inc[hd], f_row[hd])).astype(BF16)
                term = _dot_nt(xb, xb) * lmask_ref[li]
                attn[hd] = term if li == 0 else attn[hd] + term
            if li == len(LEVELS) // 2:
                tick()
        v_f32 = {hd: proj_ref[:, OFF_IA + hd * DK_A:OFF_IA + (hd + 1) * DK_A] for hd in hds}
        v_h = {hd: v_f32[hd].astype(BF16) for hd in hds}
        st = {hd: st_ref[hd] for hd in hds}
        o = {hd: _dot(attn[hd].astype(BF16), v_h[hd]) + diag[hd] * v_f32[hd] for hd in hds}
        for hd in hds:
            q_dec = (qf_h[hd] * jnp.exp2(g_inc[hd])).astype(BF16)
            o[hd] = o[hd] + _dot_nt(q_dec, st[hd].astype(BF16))
        for hd in hds:
            g_tot = g_inc[hd][TILE - 1:TILE]
            k_dec = (kk_h[hd] * jnp.exp2(g_tot - g_inc[hd])).astype(BF16)
            st_ref[hd] = jnp.exp2(g_tot) * st[hd] + _dot_tn(v_h[hd], k_dec)
        for _ in range(len(hds) - 1):
            tick()
        for hd in hds:
            ya = o[hd] * lax.rsqrt(jnp.mean(o[hd] * o[hd], axis=-1, keepdims=True) + RMS_EPS) * hg_ref[...]
            ga = proj_ref[:, OFF_GA + hd * DK_A:OFF_GA + (hd + 1) * DK_A]
            mixed_ref[:, hd * DK_A:(hd + 1) * DK_A] = (ya * _silu(ga)).astype(BF16)

    for hd0 in range(0, HA, HGRN_GROUP):
        hgrn_heads(tuple(range(hd0, hd0 + HGRN_GROUP)))

    lane = lax.broadcasted_iota(jnp.int32, (1, LANES), 1)
    lo_half = lane < DH_B
    first = (lane & (DH_B - 1)) < DH_B // 2
    cos = cos_ref[...]
    sin = sin_ref[...]

    def norm_rope(x, g):
        sq = x * x
        s0 = jnp.sum(jnp.where(lo_half, sq, 0.0), axis=-1, keepdims=True)
        s1 = jnp.sum(jnp.where(lo_half, 0.0, sq), axis=-1, keepdims=True)
        msq = jnp.where(lo_half, s0, s1) * (1.0 / DH_B)
        y = x * lax.rsqrt(msq + RMS_EPS) * g
        rot = jnp.where(first, pltpu.roll(y, LANES - DH_B // 2, axis=1),
                        pltpu.roll(y, DH_B // 2, axis=1))
        return y * cos + rot * sin

    kc = norm_rope(proj_ref[:, OFF_KB:OFF_KB + KV_WIDTH], kg_ref[...])
    vc = proj_ref[:, OFF_VB:OFF_VB + KV_WIDTH]
    is_meta_tile = blk == 0
    for buf, val in ((kbuf_ref, kc), (ksw_ref, pltpu.roll(kc, DH_B, axis=1)),
                     (vbuf_ref, vc), (vsw_ref, pltpu.roll(vc, DH_B, axis=1))):
        val = val.astype(BF16)
        buf[KB_CUR:KB_CUR + TILE, :] = val
        for m0 in (KB_META0, KB_META1):
            buf[m0:m0 + N_META, :] = jnp.where(is_meta_tile, val[PAD_FRONT:], buf[m0:m0 + N_META, :])
    tick()

    scale = DH_B ** -0.5 * LOG2E
    qcols = []
    for j in range(HB // 2):
        qj = norm_rope(proj_ref[:, OFF_QB + j * LANES:OFF_QB + (j + 1) * LANES], qg_ref[...]) * scale
        qcols.append((jnp.where(lo_half, qj, 0.0).astype(BF16), jnp.where(lo_half, 0.0, qj).astype(BF16)))
    tick()

    grp = HB // KVH_B
    heads_same = [hh for hh in range(HB) if (hh % 2) == (hh // grp)]
    heads_swap = [hh for hh in range(HB) if (hh % 2) != (hh // grp)]
    col = lax.broadcasted_iota(jnp.int32, (1, KWIN), 1)
    big = jnp.int32(1 << 30)
    windows = ((0, 0), (KBUF_ROWS - KWIN, KWIN - 1))

    def attn_sub_block(sb):
        w0, sink_col = windows[sb]
        q0 = sb * QB
        brow = w0 + col
        in_meta0 = (brow >= KB_META0) & (brow < KB_PREV)
        in_meta1 = (brow >= KB_META1) & (brow < KB_META1 + N_META)
        in_band = (brow >= KB_PREV) & (brow < KB_META1)
        kj = jnp.where(in_meta0, PAD_FRONT + brow - KB_META0,
                       jnp.where(in_meta1, PAD_FRONT + brow - KB_META1,
                                 jnp.where(in_band, (blk - 1) * TILE + brow - KB_PREV, big)))
        qi = blk * TILE + q0 + lax.broadcasted_iota(jnp.int32, (QB, 1), 0)
        allowed = (kj <= qi) & (in_meta0 | in_meta1 | ((kj > qi - TILE) & (kj >= TILE)))

        groups = ((heads_same, kbuf_ref, vbuf_ref), (heads_swap, ksw_ref, vsw_ref))
        s_all = []
        for heads, k_ref, _ in groups:
            lhs = jnp.concatenate([qcols[hh // 2][hh % 2][q0:q0 + QB] for hh in heads], axis=0)
            s_all.append(_dot_nt(lhs, k_ref[w0:w0 + KWIN, :]))
        ps, inv_den = [[] for _ in groups], [[] for _ in groups]
        for n in range(len(heads_same)):
            for g, (heads, _, _) in enumerate(groups):
                fill = jnp.where(col == sink_col, sinks_ref[heads[n]] * LOG2E, NEG_INF)
                s = jnp.where(allowed, s_all[g][n * QB:(n + 1) * QB], fill)
                pr = jnp.exp2(s - jnp.max(s, axis=-1, keepdims=True))
                inv_den[g].append(1.0 / jnp.sum(pr, axis=-1, keepdims=True))
                ps[g].append(pr.astype(BF16))
            if n == len(heads_same) // 2 - 1:
                tick()
        outs = {}
        for g, (heads, _, v_ref) in enumerate(groups):
            o_all = _dot(jnp.concatenate(ps[g], axis=0), v_ref[w0:w0 + KWIN, :])
            for n, hh in enumerate(heads):
                outs[hh] = o_all[n * QB:(n + 1) * QB] * inv_den[g][n]
        tick()
        for j in range(HB // 2):
            yb = jnp.where(lo_half, outs[2 * j], outs[2 * j + 1])
            gb = proj_ref[q0:q0 + QB, OFF_GB + j * LANES:OFF_GB + (j + 1) * LANES]
            mixed_ref[q0:q0 + QB, A_WIDTH + j * LANES:A_WIDTH + (j + 1) * LANES] = (
                yb * _silu(gb)).astype(BF16)

    for sb in range(len(windows)):
        attn_sub_block(sb)

    for buf in (kbuf_ref, ksw_ref, vbuf_ref, vsw_ref):
        buf[KB_PREV:KB_CUR, :] = buf[KB_CUR:KB_CUR + TILE, :]

    wmax = max(POOL_WINDOWS)
    ug = proj_ref[:, OFF_UC:OFF_UC + C_WIDTH] * valid
    uext_ref[wmax:wmax + TILE, :] = ug
    n_valid = jnp.maximum(row - (PAD_FRONT - 1), 0)
    for g, w in enumerate(POOL_WINDOWS):
        cs = slice(g * CG_C, (g + 1) * CG_C)
        acc = uext_ref[wmax:wmax + TILE, cs]
        for s in range(1, w):
            acc = acc + uext_ref[wmax - s:wmax - s + TILE, cs]
        cnt = (n_valid - jnp.maximum(row - w - (PAD_FRONT - 1), 0)).astype(F32)
        pooled = (acc / jnp.maximum(cnt, 1.0) - ug[:, cs]) * valid
        y = _dot(pooled.astype(BF16), pw_ref[g]) * ps_ref[:, cs]
        gc = proj_ref[:, OFF_GC + g * CG_C:OFF_GC + (g + 1) * CG_C]
        mixed_ref[:, A_WIDTH + B_WIDTH + g * CG_C:A_WIDTH + B_WIDTH + (g + 1) * CG_C] = (
            y * _silu(gc)).astype(BF16)
        tick()
    uext_ref[0:wmax, :] = ug[TILE - wmax:]


def _layer_kernel(layer, first, sinks_ref, *refs):
    n_h = 4 if first else 2
    h_refs, refs = refs[:n_h], refs[n_h:]
    (cosb_ref, sinb_ref, cosa_ref, sina_ref, ng_ref, win_hbm, lbl_ref, qg_ref, kg_ref, hg_ref, pw_ref,
     ps_ref, wout_hbm, mstack_ref, lmask_ref, out_ref, win_ref, wout_ref, wsem, xn_ref, proja_ref,
     projb_ref, mixeda_ref, mixedb_ref, expo_ref, st_ref, kbuf_ref, ksw_ref, vbuf_ref, vsw_ref,
     uext_ref) = refs
    t = pl.program_id(0)

    @pl.when(t == 0)
    def _load_weights_and_zero_state():
        stage = (proja_ref, projb_ref)
        for w_hbm, w_vmem in ((win_hbm, win_ref), (wout_hbm, wout_ref)):
            width = w_vmem.shape[1]
            n_chunks = w_vmem.shape[0] // TILE

            def chunk_copy(k, slot, w_hbm=w_hbm, width=width):
                return pltpu.make_async_copy(w_hbm.at[layer, pl.ds(k * TILE, TILE), :],
                                             stage[slot].at[:, 0:width], wsem.at[slot])

            chunk_copy(0, 0).start()
            chunk_copy(1, 1).start()

            def pair(i, carry, w_vmem=w_vmem, width=width, n_chunks=n_chunks, chunk_copy=chunk_copy):
                for slot in (0, 1):
                    k = 2 * i + slot
                    chunk_copy(k, slot).wait()
                    w_vmem[pl.ds(pl.multiple_of(k * TILE, TILE), TILE), :] = stage[slot][:, 0:width].astype(BF16)

                    @pl.when(k + 2 < n_chunks)
                    def _next():
                        chunk_copy(k + 2, slot).start()
                return carry

            lax.fori_loop(0, n_chunks // 2, pair, 0)

        for ref in (proja_ref, projb_ref, mixeda_ref, mixedb_ref, st_ref, kbuf_ref, ksw_ref,
                    vbuf_ref, vsw_ref, uext_ref):
            ref[...] = jnp.zeros_like(ref)

    def step_rows(step_idx, rws, cols, x_ref, meta_ref=None):
        x = x_ref[rws, cols]
        if not first:
            return x
        if rws.stop is not None and rws.stop < STEP:
            return jnp.where(step_idx == 0, 0.0, x)
        n_rows = x.shape[0]
        front = jnp.concatenate([jnp.zeros((n_rows - N_META, x.shape[1]), F32), meta_ref[:, cols]], axis=0)
        return jnp.where(step_idx == 0, front, x)

    d_model = out_ref.shape[1]
    last_step = pl.num_programs(0) - 1
    rows_a, rows_b = slice(0, TILE), slice(TILE, STEP)

    def norm_rows():
        h = step_rows(t, slice(None), slice(None), *h_refs[:n_h // 2])
        ms = jnp.mean(h * h, axis=-1, keepdims=True)
        xn_ref[...] = (h * lax.rsqrt(ms + RMS_EPS) * ng_ref[...]).astype(BF16)

    def out_chunk(rws, mixed_src, c0):
        cols = slice(c0, c0 + MM_CHUNK)
        h_res = step_rows(t - 1, rws, cols, *h_refs[n_h // 2:])
        out_ref[rws, cols] = h_res + _dot(mixed_src[...], wout_ref[:, cols])

    def in_chunk(rws, proj_dst, c0):
        cols = slice(c0, c0 + MM_CHUNK)
        proj_dst[:, cols] = _dot(xn_ref[rws, :], win_ref[:, cols])

    def out_chunks(rws, mixed_src):
        return [functools.partial(out_chunk, rws, mixed_src, c0) for c0 in range(0, d_model, MM_CHUNK)]

    def in_chunks(rws, proj_dst):
        return [functools.partial(in_chunk, rws, proj_dst, c0) for c0 in range(0, IN_COLS, MM_CHUNK)]

    def half_step(work, blk_mix, trig, proj_mix, mixed_mix):
        progress = {"ticks": 0, "done": 0}

        def tick():
            progress["ticks"] += 1
            pairs = -(-len(work) // 2)
            target = min(len(work), 2 * (-(-pairs * progress["ticks"] // (MIX_TICKS + 1))))
            while progress["done"] < target:
                work[progress["done"]]()
                progress["done"] += 1

        tick()
        _mix_tile(layer, blk_mix, sinks_ref, trig[0], trig[1], lbl_ref, qg_ref, kg_ref, hg_ref,
                  pw_ref, ps_ref, mstack_ref, lmask_ref, proj_mix, mixed_mix, expo_ref, st_ref,
                  kbuf_ref, ksw_ref, vbuf_ref, vsw_ref, uext_ref, tick)
        assert progress["done"] == len(work), "MIX_TICKS must match the tick() calls in _mix_tile"

    trig_a, trig_b = (cosa_ref, sina_ref), (cosb_ref, sinb_ref)

    @pl.when(t == 0)
    def _first_step():
        norm_rows()
        for chunk in in_chunks(rows_b, projb_ref):
            chunk()

    @pl.when((t > 0) & (t < last_step))
    def _main_step():
        norm_rows()
        half_step(out_chunks(rows_a, mixeda_ref) + in_chunks(rows_a, proja_ref), 2 * t - 2, trig_b,
                  projb_ref, mixedb_ref)
        half_step(out_chunks(rows_b, mixedb_ref) + in_chunks(rows_b, projb_ref), 2 * t - 1, trig_a,
                  proja_ref, mixeda_ref)

    @pl.when(t == last_step)
    def _drain_step():
        half_step(out_chunks(rows_a, mixeda_ref), 2 * t - 2, trig_b, projb_ref, mixedb_ref)
        for chunk in out_chunks(rows_b, mixedb_ref):
            chunk()


def _layer_call(layer, first, last, h_in, meta, cos, sin, sinks, norm_g, w_in, lb_logits, qg, kg, hg,
                pool_w, pool_scale, w_out, mstack, lmask):
    d = h_in.shape[1]
    n_steps = cos.shape[0] // STEP

    def rows(width, lag, skip_front=False):
        off = lag + (1 if skip_front else 0)
        hi = n_steps - 1 - (1 if skip_front else 0)
        return pl.BlockSpec((STEP, width), lambda t: (jnp.clip(t - off, 0, hi), 0))

    def tile_rows(width, tile_off):
        return pl.BlockSpec((TILE, width), lambda t: (jnp.clip(2 * t + tile_off, 0, 2 * n_steps - 1), 0))

    def whole(shape, single=False):
        idx = lambda t: (0,) * len(shape)
        if single:
            return pl.BlockSpec(shape, idx, pipeline_mode=pl.Buffered(1))
        return pl.BlockSpec(shape, idx)

    in_specs = [pl.BlockSpec(memory_space=pltpu.SMEM)]
    args = [sinks]
    for lag in (0, 1):
        in_specs.append(rows(d, lag, skip_front=first))
        args.append(h_in)
        if first:
            in_specs.append(whole(meta.shape))
            args.append(meta)
    in_specs += [
        tile_rows(LANES, -1), tile_rows(LANES, -1),
        tile_rows(LANES, 0), tile_rows(LANES, 0),
        whole(norm_g.shape),
        pl.BlockSpec(memory_space=pl.ANY),
        whole(lb_logits.shape),
        whole(qg.shape), whole(kg.shape), whole(hg.shape),
        whole(pool_w.shape), whole(pool_scale.shape),
        pl.BlockSpec(memory_space=pl.ANY),
        whole(mstack.shape, single=True), whole(lmask.shape, single=True),
    ]
    args += [cos, sin, cos, sin, norm_g, w_in, lb_logits, qg, kg, hg, pool_w, pool_scale, w_out, mstack, lmask]
    out_rows = (n_steps - 1) * STEP if last else n_steps * STEP
    assert w_in.shape[1] % (2 * TILE) == 0 and w_out.shape[1] % (2 * TILE) == 0
    assert w_out.shape[2] <= IN_COLS
    scratch = [
        pltpu.VMEM(w_in.shape[1:], BF16),
        pltpu.VMEM(w_out.shape[1:], BF16),
        pltpu.SemaphoreType.DMA((2,)),
        pltpu.VMEM((STEP, d), BF16),
        pltpu.VMEM((TILE, IN_COLS), F32),
        pltpu.VMEM((TILE, IN_COLS), F32),
        pltpu.VMEM((TILE, MIX_WIDTH), BF16),
        pltpu.VMEM((TILE, MIX_WIDTH), BF16),
        pltpu.VMEM((EXPO_BLOCKS * TILE, A_WIDTH), F32),
        pltpu.VMEM((HA, DK_A, DK_A), F32),
        pltpu.VMEM((KBUF_ROWS, LANES), BF16),
        pltpu.VMEM((KBUF_ROWS, LANES), BF16),
        pltpu.VMEM((KBUF_ROWS, LANES), BF16),
        pltpu.VMEM((KBUF_ROWS, LANES), BF16),
        pltpu.VMEM((max(POOL_WINDOWS) + TILE, C_WIDTH), F32),
    ]
    return pl.pallas_call(
        functools.partial(_layer_kernel, layer, first),
        grid=(n_steps + 1,),
        in_specs=in_specs,
        out_specs=rows(d, 1, skip_front=last),
        out_shape=jax.ShapeDtypeStruct((out_rows, d), F32),
        scratch_shapes=scratch,
        compiler_params=pltpu.CompilerParams(
            dimension_semantics=("arbitrary",), vmem_limit_bytes=VMEM_LIMIT),
        name=f"hybrid_layer{layer}",
    )(*args)


def kernel(x, meta_tokens, lb_logits, norm_g, w_in, q_norm_g, k_norm_g, attn_sinks, hgrn_norm_g,
           pool_w, pool_scale, w_out):
    b, seq, d = x.shape
    depth = w_in.shape[0]
    assert b == 1 and seq % STEP == 0
    assert w_in.shape[2] == IN_COLS and w_out.shape[1] == MIX_WIDTH
    assert meta_tokens.shape[0] == N_META

    p = STEP + seq
    pos = (jnp.arange(p) - (STEP - N_META)).astype(F32)
    half = DH_B // 2
    inv = jnp.power(ROPE_THETA, -jnp.arange(half, dtype=F32) * 2.0 / DH_B)
    ang = pos[:, None] * inv[None, :]
    cos = jnp.tile(jnp.cos(ang), (1, LANES // half))
    sin = jnp.tile(jnp.concatenate([-jnp.sin(ang), jnp.sin(ang)], axis=1), (1, LANES // DH_B))

    mstack_np, lmask_np = _decay_constants()
    mstack = jnp.asarray(mstack_np, BF16)
    lmask = jnp.asarray(lmask_np, F32)

    h = x[0]
    for l in range(depth):
        h = _layer_call(
            l, l == 0, l == depth - 1, h, meta_tokens.astype(F32), cos, sin, attn_sinks[l],
            norm_g[l][None, :], w_in, lb_logits,
            jnp.tile(q_norm_g[l], LANES // DH_B)[None, :], jnp.tile(k_norm_g[l], LANES // DH_B)[None, :],
            hgrn_norm_g[l][None, :], pool_w[l].astype(BF16), pool_scale[l][None, :],
            w_out, mstack, lmask)
    return h[None]
```

```python
import functools

import numpy as np
import jax
import jax.numpy as jnp
from jax import lax
from jax.experimental import pallas as pl
from jax.experimental.pallas import tpu as pltpu

F32 = jnp.float32
BF16 = jnp.bfloat16

N_META = 16
TILE = 128
PAD_FRONT = TILE - N_META
STEP = 2 * TILE
RMS_EPS = 1e-6
NEG_INF = -1e30
LOG_FLOOR = 1e-30
ROPE_THETA = 10000.0
LOG2E = 1.4426950408889634

HA, DK_A = 4, 128
HGRN_GROUP = 2
A_WIDTH = HA * DK_A
HB, KVH_B, DH_B = 16, 2, 64
B_WIDTH = HB * DH_B
KV_WIDTH = KVH_B * DH_B
POOL_WINDOWS = (2, 4, 8, 16)
CG_C = 128
C_WIDTH = len(POOL_WINDOWS) * CG_C
MIX_WIDTH = A_WIDTH + B_WIDTH + C_WIDTH

OFF_QA = 0
OFF_FA = OFF_QA + A_WIDTH
OFF_IA = OFF_FA + A_WIDTH
OFF_GA = OFF_IA + A_WIDTH
OFF_QB = OFF_GA + A_WIDTH
OFF_KB = OFF_QB + B_WIDTH
OFF_VB = OFF_KB + KV_WIDTH
OFF_GB = OFF_VB + KV_WIDTH
OFF_UC = OFF_GB + B_WIDTH
OFF_GC = OFF_UC + C_WIDTH
IN_COLS = OFF_GC + C_WIDTH
MM_CHUNK = 256

LANES = 128
QB = 64
KWIN = 256
KB_META0 = KWIN - N_META - TILE - QB
KB_PREV = KB_META0 + N_META
KB_CUR = KB_PREV + TILE
KB_META1 = KB_CUR + TILE
KBUF_ROWS = KB_PREV + QB + KWIN
assert KB_META0 >= 16 and KBUF_ROWS - KB_META1 - N_META >= 16
LEVELS = (128, 64, 32, 16, 8, 4, 2)
SUBLANES = 8
EXPO_BLOCKS = 2
MIX_TICKS = 1 + HA + 2 + 2 * (TILE // QB) + len(POOL_WINDOWS)
V7X_VMEM_BYTES = 64 * 1024 * 1024
VMEM_LIMIT = V7X_VMEM_BYTES - 4 * 1024 * 1024


def _decay_constants():
    r = np.arange(TILE)
    masks = []
    for blk in LEVELS:
        half = blk // 2
        pos = r % blk
        same = (r[:, None] // blk) == (r[None, :] // blk)
        masks.append((same & (pos[:, None] >= half) & (pos[None, :] < half)).astype(np.float32))
    tril = np.tril(np.ones((TILE, TILE), np.float32))
    return np.concatenate([tril, tril], axis=1), np.stack(masks, axis=0)


def _dot(a, b):
    return jnp.dot(a, b, preferred_element_type=F32)


def _dot_nt(a, b):
    return lax.dot_general(a, b, (((1,), (1,)), ((), ())), preferred_element_type=F32)


def _dot_tn(a, b):
    return lax.dot_general(a, b, (((0,), (0,)), ((), ())), preferred_element_type=F32)


def _sigmoid(x):
    return 0.5 * jnp.tanh(0.5 * x) + 0.5


def _silu(x):
    t = 0.5 * x
    return t * jnp.tanh(t) + t


def _mix_tile(layer, blk, sinks_ref, cos_ref, sin_ref, trig_rows, lbl_ref, qg_ref, kg_ref, hg_ref,
              pw_ref, ps_ref, mstack_ref, lmask_ref, proj_ref, mixed_ref, expo_ref, st_ref,
              kbuf_ref, ksw_ref, vbuf_ref, vsw_ref, uext_ref, tick):
    lyr = slice(layer, layer + 1)
    rloc = lax.broadcasted_iota(jnp.int32, (TILE, 1), 0)
    row = blk * TILE + rloc
    valid = (row >= PAD_FRONT).astype(F32)

    rows = [lbl_ref[j:j + 1, :] for j in range(lbl_ref.shape[0])]
    mx = functools.reduce(jnp.maximum, rows)
    es = [jnp.exp(x - mx) for x in rows]
    lb = sum(es[1:layer + 1], jnp.zeros_like(mx)) / sum(es)

    z = proj_ref[:, OFF_FA:OFF_FA + A_WIDTH]
    sig = _sigmoid(z)
    f = lb + (1.0 - lb) * sig
    logf = jnp.log2(jnp.maximum(f, LOG_FLOOR)) * valid
    kk = (1.0 - lb) * (1.0 - sig) * valid
    proj_ref[:, OFF_QA:OFF_QA + A_WIDTH] = _silu(proj_ref[:, OFF_QA:OFF_QA + A_WIDTH])
    proj_ref[:, OFF_FA:OFF_FA + A_WIDTH] = kk
    lf_hi = logf.astype(BF16)
    lf_lo = (logf - lf_hi.astype(F32)).astype(BF16)
    expo_ref[0:TILE, :] = _dot(mstack_ref[...], jnp.concatenate([lf_hi, lf_lo], axis=0))
    expo_ref[TILE:2 * TILE, :] = jnp.where(valid > 0.0, jnp.maximum(f, LOG_FLOOR), 1.0)
    tick()

    def level_factors(bsz, g, f_row):
        half = bsz // 2
        if bsz == 2:
            return jnp.where((rloc & 1) == 1, f_row, 1.0)
        if bsz <= SUBLANES:
            sub = lax.broadcasted_iota(jnp.int32, (SUBLANES, 1), 0)
            vregs = []
            for r0 in range(0, TILE, SUBLANES):
                gv = g[r0:r0 + SUBLANES]
                piv = None
                for b0 in range(0, SUBLANES, bsz):
                    row = jnp.broadcast_to(gv[b0 + half - 1:b0 + half, :], gv.shape)
                    piv = row if piv is None else jnp.where(sub >= b0, row, piv)
                vregs.append(gv - piv)
            d = jnp.concatenate(vregs, axis=0)
            return jnp.exp2(jnp.where((rloc & (bsz - 1)) >= half, d, -d))
        parts = []
        for b0 in range(0, TILE, bsz):
            piv = jnp.broadcast_to(g[b0 + half - 1:b0 + half, :], (half, g.shape[1]))
            parts += [piv - g[b0:b0 + half], g[b0 + half:b0 + bsz] - piv]
        return jnp.exp2(jnp.concatenate(parts, axis=0))

    def hgrn_heads(hds):
        cs = {hd: slice(hd * DK_A, (hd + 1) * DK_A) for hd in hds}
        qf_h = {hd: proj_ref[:, OFF_QA + hd * DK_A:OFF_QA + (hd + 1) * DK_A] for hd in hds}
        kk_h = {hd: proj_ref[:, OFF_FA + hd * DK_A:OFF_FA + (hd + 1) * DK_A] for hd in hds}
        diag = {hd: jnp.sum(qf_h[hd] * kk_h[hd], axis=-1, keepdims=True) for hd in hds}
        attn = {}
        g_inc = {hd: expo_ref[0:TILE, cs[hd]] for hd in hds}
        f_row = {hd: expo_ref[TILE:2 * TILE, cs[hd]] for hd in hds}
        for li, bsz in enumerate(LEVELS):
            upper = (rloc & (bsz - 1)) >= (bsz // 2)
            for hd in hds:
                xb = (jnp.where(upper, qf_h[hd], kk_h[hd])
                      * level_factors(bsz, g_inc[hd], f_row[hd])).astype(BF16)
                term = _dot_nt(xb, xb) * lmask_ref[li]
                attn[hd] = term if li == 0 else attn[hd] + term
            if li == len(LEVELS) // 2:
                tick()
        v_f32 = {hd: proj_ref[:, OFF_IA + hd * DK_A:OFF_IA + (hd + 1) * DK_A] for hd in hds}
        v_h = {hd: v_f32[hd].astype(BF16) for hd in hds}
        st = {hd: st_ref[hd] for hd in hds}
        o = {hd: _dot(attn[hd].astype(BF16), v_h[hd]) + diag[hd] * v_f32[hd] for hd in hds}
        for hd in hds:
            q_dec = (qf_h[hd] * jnp.exp2(g_inc[hd])).astype(BF16)
            o[hd] = o[hd] + _dot_nt(q_dec, st[hd].astype(BF16))
        for hd in hds:
            g_tot = g_inc[hd][TILE - 1:TILE]
            k_dec = (kk_h[hd] * jnp.exp2(g_tot - g_inc[hd])).astype(BF16)
            st_ref[hd] = jnp.exp2(g_tot) * st[hd] + _dot_tn(v_h[hd], k_dec)
        for _ in range(len(hds) - 1):
            tick()
        for hd in hds:
            ya = o[hd] * lax.rsqrt(jnp.mean(o[hd] * o[hd], axis=-1, keepdims=True) + RMS_EPS) * hg_ref[lyr, :]
            ga = proj_ref[:, OFF_GA + hd * DK_A:OFF_GA + (hd + 1) * DK_A]
            mixed_ref[:, hd * DK_A:(hd + 1) * DK_A] = (ya * _silu(ga)).astype(BF16)

    for hd0 in range(0, HA, HGRN_GROUP):
        hgrn_heads(tuple(range(hd0, hd0 + HGRN_GROUP)))

    lane = lax.broadcasted_iota(jnp.int32, (1, LANES), 1)
    lo_half = lane < DH_B
    first = (lane & (DH_B - 1)) < DH_B // 2
    cos = cos_ref[trig_rows, :]
    sin = sin_ref[trig_rows, :]

    def norm_rope(x, g):
        sq = x * x
        s0 = jnp.sum(jnp.where(lo_half, sq, 0.0), axis=-1, keepdims=True)
        s1 = jnp.sum(jnp.where(lo_half, 0.0, sq), axis=-1, keepdims=True)
        msq = jnp.where(lo_half, s0, s1) * (1.0 / DH_B)
        y = x * lax.rsqrt(msq + RMS_EPS) * g
        rot = jnp.where(first, pltpu.roll(y, LANES - DH_B // 2, axis=1),
                        pltpu.roll(y, DH_B // 2, axis=1))
        return y * cos + rot * sin

    kc = norm_rope(proj_ref[:, OFF_KB:OFF_KB + KV_WIDTH], kg_ref[lyr, :])
    vc = proj_ref[:, OFF_VB:OFF_VB + KV_WIDTH]
    is_meta_tile = blk == 0
    for buf, val in ((kbuf_ref, kc), (ksw_ref, pltpu.roll(kc, DH_B, axis=1)),
                     (vbuf_ref, vc), (vsw_ref, pltpu.roll(vc, DH_B, axis=1))):
        val = val.astype(BF16)
        buf[KB_CUR:KB_CUR + TILE, :] = val
        for m0 in (KB_META0, KB_META1):
            buf[m0:m0 + N_META, :] = jnp.where(is_meta_tile, val[PAD_FRONT:], buf[m0:m0 + N_META, :])
    tick()

    scale = DH_B ** -0.5 * LOG2E
    qcols = []
    for j in range(HB // 2):
        qj = norm_rope(proj_ref[:, OFF_QB + j * LANES:OFF_QB + (j + 1) * LANES], qg_ref[lyr, :]) * scale
        qcols.append((jnp.where(lo_half, qj, 0.0).astype(BF16), jnp.where(lo_half, 0.0, qj).astype(BF16)))
    tick()

    grp = HB // KVH_B
    heads_same = [hh for hh in range(HB) if (hh % 2) == (hh // grp)]
    heads_swap = [hh for hh in range(HB) if (hh % 2) != (hh // grp)]
    col = lax.broadcasted_iota(jnp.int32, (1, KWIN), 1)
    big = jnp.int32(1 << 30)
    windows = ((0, 0), (KBUF_ROWS - KWIN, KWIN - 1))

    def attn_sub_block(sb):
        w0, sink_col = windows[sb]
        q0 = sb * QB
        brow = w0 + col
        in_meta0 = (brow >= KB_META0) & (brow < KB_PREV)
        in_meta1 = (brow >= KB_META1) & (brow < KB_META1 + N_META)
        in_band = (brow >= KB_PREV) & (brow < KB_META1)
        kj = jnp.where(in_meta0, PAD_FRONT + brow - KB_META0,
                       jnp.where(in_meta1, PAD_FRONT + brow - KB_META1,
                                 jnp.where(in_band, (blk - 1) * TILE + brow - KB_PREV, big)))
        qi = blk * TILE + q0 + lax.broadcasted_iota(jnp.int32, (QB, 1), 0)
        allowed = (kj <= qi) & (in_meta0 | in_meta1 | ((kj > qi - TILE) & (kj >= TILE)))

        groups = ((heads_same, kbuf_ref, vbuf_ref), (heads_swap, ksw_ref, vsw_ref))
        s_all = []
        for heads, k_ref, _ in groups:
            lhs = jnp.concatenate([qcols[hh // 2][hh % 2][q0:q0 + QB] for hh in heads], axis=0)
            s_all.append(_dot_nt(lhs, k_ref[w0:w0 + KWIN, :]))
        ps, inv_den = [[] for _ in groups], [[] for _ in groups]
        for n in range(len(heads_same)):
            for g, (heads, _, _) in enumerate(groups):
                fill = jnp.where(col == sink_col, sinks_ref[layer, heads[n]] * LOG2E, NEG_INF)
                s = jnp.where(allowed, s_all[g][n * QB:(n + 1) * QB], fill)
                pr = jnp.exp2(s - jnp.max(s, axis=-1, keepdims=True))
                inv_den[g].append(1.0 / jnp.sum(pr, axis=-1, keepdims=True))
                ps[g].append(pr.astype(BF16))
            if n == len(heads_same) // 2 - 1:
                tick()
        outs = {}
        for g, (heads, _, v_ref) in enumerate(groups):
            o_all = _dot(jnp.concatenate(ps[g], axis=0), v_ref[w0:w0 + KWIN, :])
            for n, hh in enumerate(heads):
                outs[hh] = o_all[n * QB:(n + 1) * QB] * inv_den[g][n]
        tick()
        for j in range(HB // 2):
            yb = jnp.where(lo_half, outs[2 * j], outs[2 * j + 1])
            gb = proj_ref[q0:q0 + QB, OFF_GB + j * LANES:OFF_GB + (j + 1) * LANES]
            mixed_ref[q0:q0 + QB, A_WIDTH + j * LANES:A_WIDTH + (j + 1) * LANES] = (
                yb * _silu(gb)).astype(BF16)

    for sb in range(len(windows)):
        attn_sub_block(sb)

    for buf in (kbuf_ref, ksw_ref, vbuf_ref, vsw_ref):
        buf[KB_PREV:KB_CUR, :] = buf[KB_CUR:KB_CUR + TILE, :]

    wmax = max(POOL_WINDOWS)
    ug = proj_ref[:, OFF_UC:OFF_UC + C_WIDTH] * valid
    uext_ref[wmax:wmax + TILE, :] = ug
    n_valid = jnp.maximum(row - (PAD_FRONT - 1), 0)
    for g, w in enumerate(POOL_WINDOWS):
        cs = slice(g * CG_C, (g + 1) * CG_C)
        acc = uext_ref[wmax:wmax + TILE, cs]
        for s in range(1, w):
            acc = acc + uext_ref[wmax - s:wmax - s + TILE, cs]
        cnt = (n_valid - jnp.maximum(row - w - (PAD_FRONT - 1), 0)).astype(F32)
        pooled = (acc / jnp.maximum(cnt, 1.0) - ug[:, cs]) * valid
        y = _dot(pooled.astype(BF16), pw_ref[layer, g]) * ps_ref[lyr, cs]
        gc = proj_ref[:, OFF_GC + g * CG_C:OFF_GC + (g + 1) * CG_C]
        mixed_ref[:, A_WIDTH + B_WIDTH + g * CG_C:A_WIDTH + B_WIDTH + (g + 1) * CG_C] = (
            y * _silu(gc)).astype(BF16)
        tick()
    uext_ref[0:wmax, :] = ug[TILE - wmax:]


def _layer_kernel(layer, first, sinks_ref, *refs):
    n_h = 4 if first else 2
    h_refs, refs = refs[:n_h], refs[n_h:]
    (cos_ref, sin_ref, ng_ref, win_hbm, lbl_ref, qg_ref, kg_ref, hg_ref, pw_ref,
     ps_ref, wout_hbm, mstack_ref, lmask_ref, out_ref, win_ref, wout_ref, wsem, xn_ref, proja_ref,
     projb_ref, mixeda_ref, mixedb_ref, expo_ref, st_ref, kbuf_ref, ksw_ref, vbuf_ref, vsw_ref,
     uext_ref) = refs
    t = pl.program_id(0)

    @pl.when(t == 0)
    def _load_weights_and_zero_state():
        stage = (proja_ref, projb_ref)
        for w_hbm, w_vmem in ((win_hbm, win_ref), (wout_hbm, wout_ref)):
            width = w_vmem.shape[1]
            n_chunks = w_vmem.shape[0] // TILE

            def chunk_copy(k, slot, w_hbm=w_hbm, width=width):
                return pltpu.make_async_copy(w_hbm.at[layer, pl.ds(k * TILE, TILE), :],
                                             stage[slot].at[:, 0:width], wsem.at[slot])

            chunk_copy(0, 0).start()
            chunk_copy(1, 1).start()

            def pair(i, carry, w_vmem=w_vmem, width=width, n_chunks=n_chunks, chunk_copy=chunk_copy):
                for slot in (0, 1):
                    k = 2 * i + slot
                    chunk_copy(k, slot).wait()
                    w_vmem[pl.ds(pl.multiple_of(k * TILE, TILE), TILE), :] = stage[slot][:, 0:width].astype(BF16)

                    @pl.when(k + 2 < n_chunks)
                    def _next():
                        chunk_copy(k + 2, slot).start()
                return carry

            lax.fori_loop(0, n_chunks // 2, pair, 0)

        for ref in (proja_ref, projb_ref, mixeda_ref, mixedb_ref, st_ref, kbuf_ref, ksw_ref,
                    vbuf_ref, vsw_ref, uext_ref):
            ref[...] = jnp.zeros_like(ref)

    def step_rows(step_idx, rws, cols, x_ref, meta_ref=None):
        x = x_ref[rws, cols]
        if not first:
            return x
        if rws.stop is not None and rws.stop < STEP:
            return jnp.where(step_idx == 0, 0.0, x)
        n_rows = x.shape[0]
        front = jnp.concatenate([jnp.zeros((n_rows - N_META, x.shape[1]), F32), meta_ref[:, cols]], axis=0)
        return jnp.where(step_idx == 0, front, x)

    d_model = out_ref.shape[1]
    last_step = pl.num_programs(0) - 1
    rows_a, rows_b = slice(0, TILE), slice(TILE, STEP)

    def norm_rows():
        h = step_rows(t, slice(None), slice(None), *h_refs[:n_h // 2])
        ms = jnp.mean(h * h, axis=-1, keepdims=True)
        xn_ref[...] = (h * lax.rsqrt(ms + RMS_EPS) * ng_ref[layer:layer + 1, :]).astype(BF16)

    def out_chunk(rws, mixed_src, c0):
        cols = slice(c0, c0 + MM_CHUNK)
        h_res = step_rows(t - 1, rws, cols, *h_refs[n_h // 2:])
        out_ref[rws, cols] = h_res + _dot(mixed_src[...], wout_ref[:, cols])

    def in_chunk(rws, proj_dst, c0):
        cols = slice(c0, c0 + MM_CHUNK)
        proj_dst[:, cols] = _dot(xn_ref[rws, :], win_ref[:, cols])

    def out_chunks(rws, mixed_src):
        return [functools.partial(out_chunk, rws, mixed_src, c0) for c0 in range(0, d_model, MM_CHUNK)]

    def in_chunks(rws, proj_dst):
        return [functools.partial(in_chunk, rws, proj_dst, c0) for c0 in range(0, IN_COLS, MM_CHUNK)]

    def half_step(work, blk_mix, trig, proj_mix, mixed_mix):
        progress = {"ticks": 0, "done": 0}

        def tick():
            progress["ticks"] += 1
            target = min(len(work), -(-len(work) * progress["ticks"] // (MIX_TICKS + 1)))
            while progress["done"] < target:
                work[progress["done"]]()
                progress["done"] += 1

        tick()
        _mix_tile(layer, blk_mix, sinks_ref, cos_ref, sin_ref, trig, lbl_ref, qg_ref, kg_ref, hg_ref,
                  pw_ref, ps_ref, mstack_ref, lmask_ref, proj_mix, mixed_mix, expo_ref, st_ref,
                  kbuf_ref, ksw_ref, vbuf_ref, vsw_ref, uext_ref, tick)
        assert progress["done"] == len(work), "MIX_TICKS must match the tick() calls in _mix_tile"

    trig_b, trig_a = rows_a, rows_b

    @pl.when(t == 0)
    def _first_step():
        norm_rows()
        for chunk in in_chunks(rows_b, projb_ref):
            chunk()

    @pl.when((t > 0) & (t < last_step))
    def _main_step():
        norm_rows()
        half_step(out_chunks(rows_a, mixeda_ref) + in_chunks(rows_a, proja_ref), 2 * t - 2, trig_b,
                  projb_ref, mixedb_ref)
        half_step(out_chunks(rows_b, mixedb_ref) + in_chunks(rows_b, projb_ref), 2 * t - 1, trig_a,
                  proja_ref, mixeda_ref)

    @pl.when(t == last_step)
    def _drain_step():
        half_step(out_chunks(rows_a, mixeda_ref), 2 * t - 2, trig_b, projb_ref, mixedb_ref)
        for chunk in out_chunks(rows_b, mixedb_ref):
            chunk()


def _layer_call(layer, first, last, h_in, meta, cos, sin, sinks, norm_g, w_in, lb_logits, qg, kg, hg,
                pool_w, pool_scale, w_out, mstack, lmask):
    d = h_in.shape[1]
    n_steps = cos.shape[0] // STEP - 1

    def rows(width, lag, skip_front=False):
        off = lag + (1 if skip_front else 0)
        hi = n_steps - 1 - (1 if skip_front else 0)
        return pl.BlockSpec((STEP, width), lambda t: (jnp.clip(t - off, 0, hi), 0))

    def whole(shape, single=False):
        idx = lambda t: (0,) * len(shape)
        if single:
            return pl.BlockSpec(shape, idx, pipeline_mode=pl.Buffered(1))
        return pl.BlockSpec(shape, idx)

    in_specs = [pl.BlockSpec(memory_space=pltpu.SMEM)]
    args = [sinks]
    for lag in (0, 1):
        in_specs.append(rows(d, lag, skip_front=first))
        args.append(h_in)
        if first:
            in_specs.append(whole(meta.shape))
            args.append(meta)
    in_specs += [
        pl.BlockSpec((STEP, LANES), lambda t: (t, 0)),
        pl.BlockSpec((STEP, LANES), lambda t: (t, 0)),
        whole(norm_g.shape),
        pl.BlockSpec(memory_space=pl.ANY),
        whole(lb_logits.shape),
        whole(qg.shape), whole(kg.shape), whole(hg.shape),
        whole(pool_w.shape), whole(pool_scale.shape),
        pl.BlockSpec(memory_space=pl.ANY),
        whole(mstack.shape, single=True), whole(lmask.shape, single=True),
    ]
    args += [cos, sin, norm_g, w_in, lb_logits, qg, kg, hg, pool_w, pool_scale, w_out, mstack, lmask]
    out_rows = (n_steps - 1) * STEP if last else n_steps * STEP
    assert w_in.shape[1] % (2 * TILE) == 0 and w_out.shape[1] % (2 * TILE) == 0
    assert w_out.shape[2] <= IN_COLS
    scratch = [
        pltpu.VMEM(w_in.shape[1:], BF16),
        pltpu.VMEM(w_out.shape[1:], BF16),
        pltpu.SemaphoreType.DMA((2,)),
        pltpu.VMEM((STEP, d), BF16),
        pltpu.VMEM((TILE, IN_COLS), F32),
        pltpu.VMEM((TILE, IN_COLS), F32),
        pltpu.VMEM((TILE, MIX_WIDTH), BF16),
        pltpu.VMEM((TILE, MIX_WIDTH), BF16),
        pltpu.VMEM((EXPO_BLOCKS * TILE, A_WIDTH), F32),
        pltpu.VMEM((HA, DK_A, DK_A), F32),
        pltpu.VMEM((KBUF_ROWS, LANES), BF16),
        pltpu.VMEM((KBUF_ROWS, LANES), BF16),
        pltpu.VMEM((KBUF_ROWS, LANES), BF16),
        pltpu.VMEM((KBUF_ROWS, LANES), BF16),
        pltpu.VMEM((max(POOL_WINDOWS) + TILE, C_WIDTH), F32),
    ]
    return pl.pallas_call(
        functools.partial(_layer_kernel, layer, first),
        grid=(n_steps + 1,),
        in_specs=in_specs,
        out_specs=rows(d, 1, skip_front=last),
        out_shape=jax.ShapeDtypeStruct((out_rows, d), F32),
        scratch_shapes=scratch,
        compiler_params=pltpu.CompilerParams(
            dimension_semantics=("arbitrary",), vmem_limit_bytes=VMEM_LIMIT),
        name=f"hybrid_layer{layer}",
    )(*args)


def kernel(x, meta_tokens, lb_logits, norm_g, w_in, q_norm_g, k_norm_g, attn_sinks, hgrn_norm_g,
           pool_w, pool_scale, w_out):
    b, seq, d = x.shape
    depth = w_in.shape[0]
    assert b == 1 and seq % STEP == 0
    assert w_in.shape[2] == IN_COLS and w_out.shape[1] == MIX_WIDTH
    assert meta_tokens.shape[0] == N_META

    p = 2 * STEP + seq
    pos = (jnp.arange(p) - TILE - (STEP - N_META)).astype(F32)
    half = DH_B // 2
    inv = jnp.power(ROPE_THETA, -jnp.arange(half, dtype=F32) * 2.0 / DH_B)
    ang = pos[:, None] * inv[None, :]
    cos = jnp.tile(jnp.cos(ang), (1, LANES // half))
    sin = jnp.tile(jnp.concatenate([-jnp.sin(ang), jnp.sin(ang)], axis=1), (1, LANES // DH_B))

    mstack_np, lmask_np = _decay_constants()
    mstack = jnp.asarray(mstack_np, BF16)
    lmask = jnp.asarray(lmask_np, F32)

    qg = jnp.tile(q_norm_g, (1, LANES // DH_B))
    kg = jnp.tile(k_norm_g, (1, LANES // DH_B))
    pool_w16 = pool_w.astype(BF16)
    h = x[0]
    for l in range(depth):
        h = _layer_call(l, l == 0, l == depth - 1, h, meta_tokens.astype(F32), cos, sin, attn_sinks,
                        norm_g, w_in, lb_logits, qg, kg, hgrn_norm_g, pool_w16, pool_scale, w_out,
                        mstack, lmask)
    return h[None]
```

```python
import functools

import numpy as np
import jax
import jax.numpy as jnp
from jax import lax
from jax.experimental import pallas as pl
from jax.experimental.pallas import tpu as pltpu

F32 = jnp.float32
BF16 = jnp.bfloat16

N_META = 16
TILE = 128
PAD_FRONT = TILE - N_META
STEP = 2 * TILE
RMS_EPS = 1e-6
NEG_INF = -1e30
LOG_FLOOR = 1e-30
ROPE_THETA = 10000.0
LOG2E = 1.4426950408889634

HA, DK_A = 4, 128
HGRN_GROUP = 2
A_WIDTH = HA * DK_A
HB, KVH_B, DH_B = 16, 2, 64
B_WIDTH = HB * DH_B
KV_WIDTH = KVH_B * DH_B
POOL_WINDOWS = (2, 4, 8, 16)
CG_C = 128
C_WIDTH = len(POOL_WINDOWS) * CG_C
MIX_WIDTH = A_WIDTH + B_WIDTH + C_WIDTH

OFF_QA = 0
OFF_FA = OFF_QA + A_WIDTH
OFF_IA = OFF_FA + A_WIDTH
OFF_GA = OFF_IA + A_WIDTH
OFF_QB = OFF_GA + A_WIDTH
OFF_KB = OFF_QB + B_WIDTH
OFF_VB = OFF_KB + KV_WIDTH
OFF_GB = OFF_VB + KV_WIDTH
OFF_UC = OFF_GB + B_WIDTH
OFF_GC = OFF_UC + C_WIDTH
IN_COLS = OFF_GC + C_WIDTH
MM_CHUNK = 256

LANES = 128
QB = 64
KWIN = 256
KB_META0 = KWIN - N_META - TILE - QB
KB_PREV = KB_META0 + N_META
KB_CUR = KB_PREV + TILE
KB_META1 = KB_CUR + TILE
KBUF_ROWS = KB_PREV + QB + KWIN
assert KB_META0 >= 16 and KBUF_ROWS - KB_META1 - N_META >= 16
LEVELS = (128, 64, 32, 16, 8, 4, 2)
SUBLANES = 8
EXPO_BLOCKS = 2
MIX_TICKS = 1 + HA + 2 + 2 * (TILE // QB) + len(POOL_WINDOWS)
V7X_VMEM_BYTES = 64 * 1024 * 1024
VMEM_LIMIT = V7X_VMEM_BYTES - 4 * 1024 * 1024


def _decay_constants():
    r = np.arange(TILE)
    masks = []
    for blk in LEVELS:
        half = blk // 2
        pos = r % blk
        same = (r[:, None] // blk) == (r[None, :] // blk)
        masks.append((same & (pos[:, None] >= half) & (pos[None, :] < half)).astype(np.float32))
    tril = np.tril(np.ones((TILE, TILE), np.float32))
    return np.concatenate([tril, tril], axis=1), np.stack(masks, axis=0)


def _dot(a, b):
    return jnp.dot(a, b, preferred_element_type=F32)


def _dot_nt(a, b):
    return lax.dot_general(a, b, (((1,), (1,)), ((), ())), preferred_element_type=F32)


def _dot_tn(a, b):
    return lax.dot_general(a, b, (((0,), (0,)), ((), ())), preferred_element_type=F32)


def _sigmoid(x):
    return 0.5 * jnp.tanh(0.5 * x) + 0.5


def _silu(x):
    t = 0.5 * x
    return t * jnp.tanh(t) + t


def _mix_tile(layer, blk, sinks_ref, cos_ref, sin_ref, trig_rows, lbl_ref, qg_ref, kg_ref, hg_ref,
              pw_ref, ps_ref, mstack_ref, lmask_ref, proj_ref, mixed_ref, expo_ref, st_ref,
              kbuf_ref, ksw_ref, vbuf_ref, vsw_ref, uext_ref, tick):
    lyr = slice(layer, layer + 1)
    rloc = lax.broadcasted_iota(jnp.int32, (TILE, 1), 0)
    row = blk * TILE + rloc
    valid = (row >= PAD_FRONT).astype(F32)

    rows = [lbl_ref[j:j + 1, :] for j in range(lbl_ref.shape[0])]
    mx = functools.reduce(jnp.maximum, rows)
    es = [jnp.exp(x - mx) for x in rows]
    lb = sum(es[1:layer + 1], jnp.zeros_like(mx)) / sum(es)

    z = proj_ref[:, OFF_FA:OFF_FA + A_WIDTH]
    sig = _sigmoid(z)
    f = lb + (1.0 - lb) * sig
    logf = jnp.log2(jnp.maximum(f, LOG_FLOOR)) * valid
    kk = (1.0 - lb) * (1.0 - sig) * valid
    proj_ref[:, OFF_QA:OFF_QA + A_WIDTH] = _silu(proj_ref[:, OFF_QA:OFF_QA + A_WIDTH])
    proj_ref[:, OFF_FA:OFF_FA + A_WIDTH] = kk
    lf_hi = logf.astype(BF16)
    lf_lo = (logf - lf_hi.astype(F32)).astype(BF16)
    expo_ref[0:TILE, :] = _dot(mstack_ref[...], jnp.concatenate([lf_hi, lf_lo], axis=0))
    expo_ref[TILE:2 * TILE, :] = jnp.where(valid > 0.0, jnp.maximum(f, LOG_FLOOR), 1.0)
    tick()

    def level_factors(bsz, g, f_row):
        half = bsz // 2
        if bsz == 2:
            return jnp.where((rloc & 1) == 1, f_row, 1.0)
        if bsz <= SUBLANES:
            sub = lax.broadcasted_iota(jnp.int32, (SUBLANES, 1), 0)
            vregs = []
            for r0 in range(0, TILE, SUBLANES):
                gv = g[r0:r0 + SUBLANES]
                piv = None
                for b0 in range(0, SUBLANES, bsz):
                    row = jnp.broadcast_to(gv[b0 + half - 1:b0 + half, :], gv.shape)
                    piv = row if piv is None else jnp.where(sub >= b0, row, piv)
                vregs.append(gv - piv)
            d = jnp.concatenate(vregs, axis=0)
            return jnp.exp2(jnp.where((rloc & (bsz - 1)) >= half, d, -d))
        parts = []
        for b0 in range(0, TILE, bsz):
            piv = jnp.broadcast_to(g[b0 + half - 1:b0 + half, :], (half, g.shape[1]))
            parts += [piv - g[b0:b0 + half], g[b0 + half:b0 + bsz] - piv]
        return jnp.exp2(jnp.concatenate(parts, axis=0))

    def hgrn_heads(hds):
        cs = {hd: slice(hd * DK_A, (hd + 1) * DK_A) for hd in hds}
        qf_h = {hd: proj_ref[:, OFF_QA + hd * DK_A:OFF_QA + (hd + 1) * DK_A] for hd in hds}
        kk_h = {hd: proj_ref[:, OFF_FA + hd * DK_A:OFF_FA + (hd + 1) * DK_A] for hd in hds}
        diag = {hd: jnp.sum(qf_h[hd] * kk_h[hd], axis=-1, keepdims=True) for hd in hds}
        attn = {}
        g_inc = {hd: expo_ref[0:TILE, cs[hd]] for hd in hds}
        f_row = {hd: expo_ref[TILE:2 * TILE, cs[hd]] for hd in hds}
        for li, bsz in enumerate(LEVELS):
            upper = (rloc & (bsz - 1)) >= (bsz // 2)
            for hd in hds:
                xb = (jnp.where(upper, qf_h[hd], kk_h[hd])
                      * level_factors(bsz, g_inc[hd], f_row[hd])).astype(BF16)
                term = _dot_nt(xb, xb) * lmask_ref[li]
                attn[hd] = term if li == 0 else attn[hd] + term
            if li == len(LEVELS) // 2:
                tick()
        v_f32 = {hd: proj_ref[:, OFF_IA + hd * DK_A:OFF_IA + (hd + 1) * DK_A] for hd in hds}
        v_h = {hd: v_f32[hd].astype(BF16) for hd in hds}
        st = {hd: st_ref[hd] for hd in hds}
        o = {hd: _dot(attn[hd].astype(BF16), v_h[hd]) + diag[hd] * v_f32[hd] for hd in hds}
        for hd in hds:
            q_dec = (qf_h[hd] * jnp.exp2(g_inc[hd])).astype(BF16)
            o[hd] = o[hd] + _dot_nt(q_dec, st[hd].astype(BF16))
        for hd in hds:
            g_tot = g_inc[hd][TILE - 1:TILE]
            k_dec = (kk_h[hd] * jnp.exp2(g_tot - g_inc[hd])).astype(BF16)
            st_ref[hd] = jnp.exp2(g_tot) * st[hd] + _dot_tn(v_h[hd], k_dec)
        for _ in range(len(hds) - 1):
            tick()
        for hd in hds:
            ya = o[hd] * lax.rsqrt(jnp.mean(o[hd] * o[hd], axis=-1, keepdims=True) + RMS_EPS) * hg_ref[lyr, :]
            ga = proj_ref[:, OFF_GA + hd * DK_A:OFF_GA + (hd + 1) * DK_A]
            mixed_ref[:, hd * DK_A:(hd + 1) * DK_A] = (ya * _silu(ga)).astype(BF16)

    for hd0 in range(0, HA, HGRN_GROUP):
        hgrn_heads(tuple(range(hd0, hd0 + HGRN_GROUP)))

    lane = lax.broadcasted_iota(jnp.int32, (1, LANES), 1)
    lo_half = lane < DH_B
    first = (lane & (DH_B - 1)) < DH_B // 2
    cos = cos_ref[trig_rows, :]
    sin = sin_ref[trig_rows, :]

    def norm_rope(x, g):
        sq = x * x
        s0 = jnp.sum(jnp.where(lo_half, sq, 0.0), axis=-1, keepdims=True)
        s1 = jnp.sum(jnp.where(lo_half, 0.0, sq), axis=-1, keepdims=True)
        msq = jnp.where(lo_half, s0, s1) * (1.0 / DH_B)
        y = x * lax.rsqrt(msq + RMS_EPS) * g
        rot = jnp.where(first, pltpu.roll(y, LANES - DH_B // 2, axis=1),
                        pltpu.roll(y, DH_B // 2, axis=1))
        return y * cos + rot * sin

    kc = norm_rope(proj_ref[:, OFF_KB:OFF_KB + KV_WIDTH], kg_ref[lyr, :])
    vc = proj_ref[:, OFF_VB:OFF_VB + KV_WIDTH]
    is_meta_tile = blk == 0
    for buf, val in ((kbuf_ref, kc), (ksw_ref, pltpu.roll(kc, DH_B, axis=1)),
                     (vbuf_ref, vc), (vsw_ref, pltpu.roll(vc, DH_B, axis=1))):
        val = val.astype(BF16)
        buf[KB_CUR:KB_CUR + TILE, :] = val
        for m0 in (KB_META0, KB_META1):
            buf[m0:m0 + N_META, :] = jnp.where(is_meta_tile, val[PAD_FRONT:], buf[m0:m0 + N_META, :])
    tick()

    scale = DH_B ** -0.5 * LOG2E
    qcols = []
    for j in range(HB // 2):
        qj = norm_rope(proj_ref[:, OFF_QB + j * LANES:OFF_QB + (j + 1) * LANES], qg_ref[lyr, :]) * scale
        qcols.append((jnp.where(lo_half, qj, 0.0).astype(BF16), jnp.where(lo_half, 0.0, qj).astype(BF16)))
    tick()

    grp = HB // KVH_B
    heads_same = [hh for hh in range(HB) if (hh % 2) == (hh // grp)]
    heads_swap = [hh for hh in range(HB) if (hh % 2) != (hh // grp)]
    col = lax.broadcasted_iota(jnp.int32, (1, KWIN), 1)
    big = jnp.int32(1 << 30)
    windows = ((0, 0), (KBUF_ROWS - KWIN, KWIN - 1))

    def attn_sub_block(sb):
        w0, sink_col = windows[sb]
        q0 = sb * QB
        brow = w0 + col
        in_meta0 = (brow >= KB_META0) & (brow < KB_PREV)
        in_meta1 = (brow >= KB_META1) & (brow < KB_META1 + N_META)
        in_band = (brow >= KB_PREV) & (brow < KB_META1)
        kj = jnp.where(in_meta0, PAD_FRONT + brow - KB_META0,
                       jnp.where(in_meta1, PAD_FRONT + brow - KB_META1,
                                 jnp.where(in_band, (blk - 1) * TILE + brow - KB_PREV, big)))
        qi = blk * TILE + q0 + lax.broadcasted_iota(jnp.int32, (QB, 1), 0)
        allowed = (kj <= qi) & (in_meta0 | in_meta1 | ((kj > qi - TILE) & (kj >= TILE)))

        groups = ((heads_same, kbuf_ref, vbuf_ref), (heads_swap, ksw_ref, vsw_ref))
        s_all = []
        for heads, k_ref, _ in groups:
            lhs = jnp.concatenate([qcols[hh // 2][hh % 2][q0:q0 + QB] for hh in heads], axis=0)
            s_all.append(_dot_nt(lhs, k_ref[w0:w0 + KWIN, :]))
        ps, inv_den = [[] for _ in groups], [[] for _ in groups]
        for n in range(len(heads_same)):
            for g, (heads, _, _) in enumerate(groups):
                fill = jnp.where(col == sink_col, sinks_ref[layer, heads[n]] * LOG2E, NEG_INF)
                s = jnp.where(allowed, s_all[g][n * QB:(n + 1) * QB], fill)
                pr = jnp.exp2(s - jnp.max(s, axis=-1, keepdims=True))
                inv_den[g].append(1.0 / jnp.sum(pr, axis=-1, keepdims=True))
                ps[g].append(pr.astype(BF16))
            if n == len(heads_same) // 2 - 1:
                tick()
        outs = {}
        for g, (heads, _, v_ref) in enumerate(groups):
            o_all = _dot(jnp.concatenate(ps[g], axis=0), v_ref[w0:w0 + KWIN, :])
            for n, hh in enumerate(heads):
                outs[hh] = o_all[n * QB:(n + 1) * QB] * inv_den[g][n]
        tick()
        for j in range(HB // 2):
            yb = jnp.where(lo_half, outs[2 * j], outs[2 * j + 1])
            gb = proj_ref[q0:q0 + QB, OFF_GB + j * LANES:OFF_GB + (j + 1) * LANES]
            mixed_ref[q0:q0 + QB, A_WIDTH + j * LANES:A_WIDTH + (j + 1) * LANES] = (
                yb * _silu(gb)).astype(BF16)

    for sb in range(len(windows)):
        attn_sub_block(sb)

    for buf in (kbuf_ref, ksw_ref, vbuf_ref, vsw_ref):
        buf[KB_PREV:KB_CUR, :] = buf[KB_CUR:KB_CUR + TILE, :]

    wmax = max(POOL_WINDOWS)
    ug = proj_ref[:, OFF_UC:OFF_UC + C_WIDTH] * valid
    uext_ref[wmax:wmax + TILE, :] = ug
    n_valid = jnp.maximum(row - (PAD_FRONT - 1), 0)
    for g, w in enumerate(POOL_WINDOWS):
        cs = slice(g * CG_C, (g + 1) * CG_C)
        acc = uext_ref[wmax:wmax + TILE, cs]
        for s in range(1, w):
            acc = acc + uext_ref[wmax - s:wmax - s + TILE, cs]
        cnt = (n_valid - jnp.maximum(row - w - (PAD_FRONT - 1), 0)).astype(F32)
        pooled = (acc / jnp.maximum(cnt, 1.0) - ug[:, cs]) * valid
        y = _dot(pooled.astype(BF16), pw_ref[layer, g]) * ps_ref[lyr, cs]
        gc = proj_ref[:, OFF_GC + g * CG_C:OFF_GC + (g + 1) * CG_C]
        mixed_ref[:, A_WIDTH + B_WIDTH + g * CG_C:A_WIDTH + B_WIDTH + (g + 1) * CG_C] = (
            y * _silu(gc)).astype(BF16)
        tick()
    uext_ref[0:wmax, :] = ug[TILE - wmax:]


def _layer_kernel(layer, first, sinks_ref, *refs):
    n_h = 4 if first else 2
    h_refs, refs = refs[:n_h], refs[n_h:]
    (cos_ref, sin_ref, ng_ref, win_hbm, lbl_ref, qg_ref, kg_ref, hg_ref, pw_ref,
     ps_ref, wout_hbm, mstack_ref, lmask_ref, out_ref, win_ref, wout_ref, wsem, xn_ref, proja_ref,
     projb_ref, mixeda_ref, mixedb_ref, expo_ref, st_ref, kbuf_ref, ksw_ref, vbuf_ref, vsw_ref,
     uext_ref) = refs
    t = pl.program_id(0)

    @pl.when(t == 0)
    def _load_weights_and_zero_state():
        stage = (proja_ref, projb_ref)
        for w_hbm, w_vmem in ((win_hbm, win_ref), (wout_hbm, wout_ref)):
            width = w_vmem.shape[1]
            n_chunks = w_vmem.shape[0] // TILE

            def chunk_copy(k, slot, w_hbm=w_hbm, width=width):
                return pltpu.make_async_copy(w_hbm.at[layer, pl.ds(k * TILE, TILE), :],
                                             stage[slot].at[:, 0:width], wsem.at[slot])

            chunk_copy(0, 0).start()
            chunk_copy(1, 1).start()

            def pair(i, carry, w_vmem=w_vmem, width=width, n_chunks=n_chunks, chunk_copy=chunk_copy):
                for slot in (0, 1):
                    k = 2 * i + slot
                    chunk_copy(k, slot).wait()
                    w_vmem[pl.ds(pl.multiple_of(k * TILE, TILE), TILE), :] = stage[slot][:, 0:width].astype(BF16)

                    @pl.when(k + 2 < n_chunks)
                    def _next():
                        chunk_copy(k + 2, slot).start()
                return carry

            lax.fori_loop(0, n_chunks // 2, pair, 0)

        for ref in (proja_ref, projb_ref, mixeda_ref, mixedb_ref, st_ref, kbuf_ref, ksw_ref,
                    vbuf_ref, vsw_ref, uext_ref):
            ref[...] = jnp.zeros_like(ref)

    def step_rows(step_idx, rws, cols, x_ref, meta_ref=None):
        x = x_ref[rws, cols]
        if not first:
            return x
        if rws.stop is not None and rws.stop < STEP:
            return jnp.where(step_idx == 0, 0.0, x)
        n_rows = x.shape[0]
        front = jnp.concatenate([jnp.zeros((n_rows - N_META, x.shape[1]), F32), meta_ref[:, cols]], axis=0)
        return jnp.where(step_idx == 0, front, x)

    d_model = out_ref.shape[1]
    last_step = pl.num_programs(0) - 1
    rows_a, rows_b = slice(0, TILE), slice(TILE, STEP)

    def norm_rows():
        h = step_rows(t, slice(None), slice(None), *h_refs[:n_h // 2])
        ms = jnp.mean(h * h, axis=-1, keepdims=True)
        xn_ref[...] = (h * lax.rsqrt(ms + RMS_EPS) * ng_ref[layer:layer + 1, :]).astype(BF16)

    def out_chunk(rws, mixed_src, c0):
        cols = slice(c0, c0 + MM_CHUNK)
        h_res = step_rows(t - 1, rws, cols, *h_refs[n_h // 2:])
        out_ref[rws, cols] = h_res + _dot(mixed_src[...], wout_ref[:, cols])

    def in_chunk(rws, proj_dst, c0):
        cols = slice(c0, c0 + MM_CHUNK)
        proj_dst[:, cols] = _dot(xn_ref[rws, :], win_ref[:, cols])

    def out_chunks(rws, mixed_src):
        return [functools.partial(out_chunk, rws, mixed_src, c0) for c0 in range(0, d_model, MM_CHUNK)]

    def in_chunks(rws, proj_dst):
        return [functools.partial(in_chunk, rws, proj_dst, c0) for c0 in range(0, IN_COLS, MM_CHUNK)]

    def half_step(work, blk_mix, trig, proj_mix, mixed_mix):
        progress = {"ticks": 0, "done": 0}

        def tick():
            progress["ticks"] += 1
            target = min(len(work), -(-len(work) * progress["ticks"] // (MIX_TICKS + 1)))
            while progress["done"] < target:
                work[progress["done"]]()
                progress["done"] += 1

        tick()
        _mix_tile(layer, blk_mix, sinks_ref, cos_ref, sin_ref, trig, lbl_ref, qg_ref, kg_ref, hg_ref,
                  pw_ref, ps_ref, mstack_ref, lmask_ref, proj_mix, mixed_mix, expo_ref, st_ref,
                  kbuf_ref, ksw_ref, vbuf_ref, vsw_ref, uext_ref, tick)
        assert progress["done"] == len(work), "MIX_TICKS must match the tick() calls in _mix_tile"

    trig_b, trig_a = rows_a, rows_b

    @pl.when(t == 0)
    def _first_step():
        norm_rows()
        for chunk in in_chunks(rows_b, projb_ref):
            chunk()

    @pl.when((t > 0) & (t < last_step))
    def _main_step():
        norm_rows()
        half_step(out_chunks(rows_a, mixeda_ref) + in_chunks(rows_a, proja_ref), 2 * t - 2, trig_b,
                  projb_ref, mixedb_ref)
        half_step(out_chunks(rows_b, mixedb_ref) + in_chunks(rows_b, projb_ref), 2 * t - 1, trig_a,
                  proja_ref, mixeda_ref)

    @pl.when(t == last_step)
    def _drain_step():
        half_step(out_chunks(rows_a, mixeda_ref), 2 * t - 2, trig_b, projb_ref, mixedb_ref)
        for chunk in out_chunks(rows_b, mixedb_ref):
            chunk()


def _layer_call(layer, first, last, h_in, meta, cos, sin, sinks, norm_g, w_in, lb_logits, qg, kg, hg,
                pool_w, pool_scale, w_out, mstack, lmask):
    d = h_in.shape[1]
    n_steps = cos.shape[0] // STEP - 1

    def rows(width, lag, skip_front=False):
        off = lag + (1 if skip_front else 0)
        hi = n_steps - 1 - (1 if skip_front else 0)
        return pl.BlockSpec((STEP, width), lambda t: (jnp.clip(t - off, 0, hi), 0))

    def whole(shape, single=False):
        idx = lambda t: (0,) * len(shape)
        if single:
            return pl.BlockSpec(shape, idx, pipeline_mode=pl.Buffered(1))
        return pl.BlockSpec(shape, idx)

    in_specs = [pl.BlockSpec(memory_space=pltpu.SMEM)]
    args = [sinks]
    for lag in (0, 1):
        in_specs.append(rows(d, lag, skip_front=first))
        args.append(h_in)
        if first:
            in_specs.append(whole(meta.shape))
            args.append(meta)
    in_specs += [
        pl.BlockSpec((STEP, LANES), lambda t: (t, 0)),
        pl.BlockSpec((STEP, LANES), lambda t: (t, 0)),
        whole(norm_g.shape),
        pl.BlockSpec(memory_space=pl.ANY),
        whole(lb_logits.shape),
        whole(qg.shape), whole(kg.shape), whole(hg.shape),
        whole(pool_w.shape), whole(pool_scale.shape),
        pl.BlockSpec(memory_space=pl.ANY),
        whole(mstack.shape, single=True), whole(lmask.shape, single=True),
    ]
    args += [cos, sin, norm_g, w_in, lb_logits, qg, kg, hg, pool_w, pool_scale, w_out, mstack, lmask]
    out_rows = (n_steps - 1) * STEP if last else n_steps * STEP
    assert w_in.shape[1] % (2 * TILE) == 0 and w_out.shape[1] % (2 * TILE) == 0
    assert w_out.shape[2] <= IN_COLS
    scratch = [
        pltpu.VMEM(w_in.shape[1:], BF16),
        pltpu.VMEM(w_out.shape[1:], BF16),
        pltpu.SemaphoreType.DMA((2,)),
        pltpu.VMEM((STEP, d), BF16),
        pltpu.VMEM((TILE, IN_COLS), F32),
        pltpu.VMEM((TILE, IN_COLS), F32),
        pltpu.VMEM((TILE, MIX_WIDTH), BF16),
        pltpu.VMEM((TILE, MIX_WIDTH), BF16),
        pltpu.VMEM((EXPO_BLOCKS * TILE, A_WIDTH), F32),
        pltpu.VMEM((HA, DK_A, DK_A), F32),
        pltpu.VMEM((KBUF_ROWS, LANES), BF16),
        pltpu.VMEM((KBUF_ROWS, LANES), BF16),
        pltpu.VMEM((KBUF_ROWS, LANES), BF16),
        pltpu.VMEM((KBUF_ROWS, LANES), BF16),
        pltpu.VMEM((max(POOL_WINDOWS) + TILE, C_WIDTH), F32),
    ]
    return pl.pallas_call(
        functools.partial(_layer_kernel, layer, first),
        grid=(n_steps + 1,),
        in_specs=in_specs,
        out_specs=rows(d, 1, skip_front=last),
        out_shape=jax.ShapeDtypeStruct((out_rows, d), F32),
        scratch_shapes=scratch,
        compiler_params=pltpu.CompilerParams(
            dimension_semantics=("arbitrary",), vmem_limit_bytes=VMEM_LIMIT),
        name=f"hybrid_layer{layer}",
    )(*args)


def kernel(x, meta_tokens, lb_logits, norm_g, w_in, q_norm_g, k_norm_g, attn_sinks, hgrn_norm_g,
           pool_w, pool_scale, w_out):
    b, seq, d = x.shape
    depth = w_in.shape[0]
    assert b == 1 and seq % STEP == 0
    assert w_in.shape[2] == IN_COLS and w_out.shape[1] == MIX_WIDTH
    assert meta_tokens.shape[0] == N_META

    p = 2 * STEP + seq
    pos = np.arange(p, dtype=np.float64) - TILE - (STEP - N_META)
    half = DH_B // 2
    inv = np.power(ROPE_THETA, -np.arange(half, dtype=np.float64) * 2.0 / DH_B)
    ang = pos[:, None] * inv[None, :]
    cos = jnp.asarray(np.tile(np.cos(ang), (1, LANES // half)), F32)
    sin = jnp.asarray(np.tile(np.concatenate([-np.sin(ang), np.sin(ang)], axis=1), (1, LANES // DH_B)), F32)

    mstack_np, lmask_np = _decay_constants()
    mstack = jnp.asarray(mstack_np, BF16)
    lmask = jnp.asarray(lmask_np, F32)

    qg = jnp.tile(q_norm_g, (1, LANES // DH_B))
    kg = jnp.tile(k_norm_g, (1, LANES // DH_B))
    pool_w16 = pool_w.astype(BF16)
    h = x[0]
    for l in range(depth):
        h = _layer_call(l, l == 0, l == depth - 1, h, meta_tokens.astype(F32), cos, sin, attn_sinks,
                        norm_g, w_in, lb_logits, qg, kg, hgrn_norm_g, pool_w16, pool_scale, w_out,
                        mstack, lmask)
    return h[None]
```

```python
import functools

import numpy as np
import jax
import jax.numpy as jnp
from jax import lax
from jax.experimental import pallas as pl
from jax.experimental.pallas import tpu as pltpu

F32 = jnp.float32
BF16 = jnp.bfloat16

N_META = 16
TILE = 128
PAD_FRONT = TILE - N_META
STEP = 2 * TILE
RMS_EPS = 1e-6
NEG_INF = -1e30
LOG_FLOOR = 1e-30
ROPE_THETA = 10000.0
LOG2E = 1.4426950408889634

HA, DK_A = 4, 128
HGRN_GROUP = 2
A_WIDTH = HA * DK_A
HB, KVH_B, DH_B = 16, 2, 64
B_WIDTH = HB * DH_B
KV_WIDTH = KVH_B * DH_B
POOL_WINDOWS = (2, 4, 8, 16)
CG_C = 128
C_WIDTH = len(POOL_WINDOWS) * CG_C
MIX_WIDTH = A_WIDTH + B_WIDTH + C_WIDTH

OFF_QA = 0
OFF_FA = OFF_QA + A_WIDTH
OFF_IA = OFF_FA + A_WIDTH
OFF_GA = OFF_IA + A_WIDTH
OFF_QB = OFF_GA + A_WIDTH
OFF_KB = OFF_QB + B_WIDTH
OFF_VB = OFF_KB + KV_WIDTH
OFF_GB = OFF_VB + KV_WIDTH
OFF_UC = OFF_GB + B_WIDTH
OFF_GC = OFF_UC + C_WIDTH
IN_COLS = OFF_GC + C_WIDTH
MM_CHUNK = 256

LANES = 128
QB = 64
KWIN = 256
KB_META0 = KWIN - N_META - TILE - QB
KB_PREV = KB_META0 + N_META
KB_CUR = KB_PREV + TILE
KB_META1 = KB_CUR + TILE
KBUF_ROWS = KB_PREV + QB + KWIN
assert KB_META0 >= 16 and KBUF_ROWS - KB_META1 - N_META >= 16
LEVELS = (128, 64, 32, 16, 8, 4, 2)
SUBLANES = 8
EXPO_BLOCKS = 2
MIX_TICKS = 1 + HA + 2 + 2 * (TILE // QB) + len(POOL_WINDOWS)
V7X_VMEM_BYTES = 64 * 1024 * 1024
VMEM_LIMIT = V7X_VMEM_BYTES - 4 * 1024 * 1024


def _decay_constants():
    r = np.arange(TILE)
    masks = []
    for blk in LEVELS:
        half = blk // 2
        pos = r % blk
        same = (r[:, None] // blk) == (r[None, :] // blk)
        masks.append((same & (pos[:, None] >= half) & (pos[None, :] < half)).astype(np.float32))
    tril = np.tril(np.ones((TILE, TILE), np.float32))
    return np.concatenate([tril, tril], axis=1), np.stack(masks, axis=0)


def _dot(a, b):
    return jnp.dot(a, b, preferred_element_type=F32)


def _dot_nt(a, b):
    return lax.dot_general(a, b, (((1,), (1,)), ((), ())), preferred_element_type=F32)


def _dot_tn(a, b):
    return lax.dot_general(a, b, (((0,), (0,)), ((), ())), preferred_element_type=F32)


def _sigmoid(x):
    return 0.5 * jnp.tanh(0.5 * x) + 0.5


def _silu(x):
    t = 0.5 * x
    return t * jnp.tanh(t) + t


def _mix_tile(layer, blk, sinks_ref, cos_ref, sin_ref, trig_rows, lbl_ref, qg_ref, kg_ref, hg_ref,
              pw_ref, ps_ref, mstack_ref, lmask_ref, proj_ref, mixed_ref, expo_ref, st_ref,
              kbuf_ref, ksw_ref, vbuf_ref, vsw_ref, uext_ref, tick):
    lyr = slice(layer, layer + 1)
    rloc = lax.broadcasted_iota(jnp.int32, (TILE, 1), 0)
    row = blk * TILE + rloc
    valid = (row >= PAD_FRONT).astype(F32)

    rows = [lbl_ref[j:j + 1, :] for j in range(lbl_ref.shape[0])]
    mx = functools.reduce(jnp.maximum, rows)
    es = [jnp.exp(x - mx) for x in rows]
    lb = sum(es[1:layer + 1], jnp.zeros_like(mx)) / sum(es)

    z = proj_ref[:, OFF_FA:OFF_FA + A_WIDTH]
    sig = _sigmoid(z)
    f = lb + (1.0 - lb) * sig
    logf = jnp.log2(jnp.maximum(f, LOG_FLOOR)) * valid
    kk = (1.0 - lb) * (1.0 - sig) * valid
    proj_ref[:, OFF_QA:OFF_QA + A_WIDTH] = _silu(proj_ref[:, OFF_QA:OFF_QA + A_WIDTH])
    proj_ref[:, OFF_FA:OFF_FA + A_WIDTH] = kk
    lf_hi = logf.astype(BF16)
    lf_lo = (logf - lf_hi.astype(F32)).astype(BF16)
    expo_ref[0:TILE, :] = _dot(mstack_ref[...], jnp.concatenate([lf_hi, lf_lo], axis=0))
    expo_ref[TILE:2 * TILE, :] = jnp.where(valid > 0.0, jnp.maximum(f, LOG_FLOOR), 1.0)
    tick()

    def level_factors(bsz, g, f_row):
        half = bsz // 2
        if bsz == 2:
            return jnp.where((rloc & 1) == 1, f_row, 1.0)
        if bsz <= SUBLANES:
            sub = lax.broadcasted_iota(jnp.int32, (SUBLANES, 1), 0)
            vregs = []
            for r0 in range(0, TILE, SUBLANES):
                gv = g[r0:r0 + SUBLANES]
                piv = None
                for b0 in range(0, SUBLANES, bsz):
                    row = jnp.broadcast_to(gv[b0 + half - 1:b0 + half, :], gv.shape)
                    piv = row if piv is None else jnp.where(sub >= b0, row, piv)
                vregs.append(gv - piv)
            d = jnp.concatenate(vregs, axis=0)
            return jnp.exp2(jnp.where((rloc & (bsz - 1)) >= half, d, -d))
        parts = []
        for b0 in range(0, TILE, bsz):
            piv = jnp.broadcast_to(g[b0 + half - 1:b0 + half, :], (half, g.shape[1]))
            parts += [piv - g[b0:b0 + half], g[b0 + half:b0 + bsz] - piv]
        return jnp.exp2(jnp.concatenate(parts, axis=0))

    def hgrn_heads(hds):
        cs = {hd: slice(hd * DK_A, (hd + 1) * DK_A) for hd in hds}
        qf_h = {hd: proj_ref[:, OFF_QA + hd * DK_A:OFF_QA + (hd + 1) * DK_A] for hd in hds}
        kk_h = {hd: proj_ref[:, OFF_FA + hd * DK_A:OFF_FA + (hd + 1) * DK_A] for hd in hds}
        diag = {hd: jnp.sum(qf_h[hd] * kk_h[hd], axis=-1, keepdims=True) for hd in hds}
        attn = {}
        g_inc = {hd: expo_ref[0:TILE, cs[hd]] for hd in hds}
        f_row = {hd: expo_ref[TILE:2 * TILE, cs[hd]] for hd in hds}
        for li, bsz in enumerate(LEVELS):
            upper = (rloc & (bsz - 1)) >= (bsz // 2)
            for hd in hds:
                xb = (jnp.where(upper, qf_h[hd], kk_h[hd])
                      * level_factors(bsz, g_inc[hd], f_row[hd])).astype(BF16)
                term = _dot_nt(xb, xb) * lmask_ref[li]
                attn[hd] = term if li == 0 else attn[hd] + term
            if li == len(LEVELS) // 2:
                tick()
        v_f32 = {hd: proj_ref[:, OFF_IA + hd * DK_A:OFF_IA + (hd + 1) * DK_A] for hd in hds}
        v_h = {hd: v_f32[hd].astype(BF16) for hd in hds}
        st = {hd: st_ref[hd] for hd in hds}
        o = {hd: _dot(attn[hd].astype(BF16), v_h[hd]) + diag[hd] * v_f32[hd] for hd in hds}
        for hd in hds:
            q_dec = (qf_h[hd] * jnp.exp2(g_inc[hd])).astype(BF16)
            o[hd] = o[hd] + _dot_nt(q_dec, st[hd].astype(BF16))
        for hd in hds:
            g_tot = g_inc[hd][TILE - 1:TILE]
            k_dec = (kk_h[hd] * jnp.exp2(g_tot - g_inc[hd])).astype(BF16)
            st_ref[hd] = jnp.exp2(g_tot) * st[hd] + _dot_tn(v_h[hd], k_dec)
        for _ in range(len(hds) - 1):
            tick()
        for hd in hds:
            ya = o[hd] * lax.rsqrt(jnp.mean(o[hd] * o[hd], axis=-1, keepdims=True) + RMS_EPS) * hg_ref[lyr, :]
            ga = proj_ref[:, OFF_GA + hd * DK_A:OFF_GA + (hd + 1) * DK_A]
            mixed_ref[:, hd * DK_A:(hd + 1) * DK_A] = (ya * _silu(ga)).astype(BF16)

    for hd0 in range(0, HA, HGRN_GROUP):
        hgrn_heads(tuple(range(hd0, hd0 + HGRN_GROUP)))

    lane = lax.broadcasted_iota(jnp.int32, (1, LANES), 1)
    lo_half = lane < DH_B
    first = (lane & (DH_B - 1)) < DH_B // 2
    cos = cos_ref[trig_rows, :]
    sin = sin_ref[trig_rows, :]

    def norm_rope(x, g):
        sq = x * x
        s0 = jnp.sum(jnp.where(lo_half, sq, 0.0), axis=-1, keepdims=True)
        s1 = jnp.sum(jnp.where(lo_half, 0.0, sq), axis=-1, keepdims=True)
        msq = jnp.where(lo_half, s0, s1) * (1.0 / DH_B)
        y = x * lax.rsqrt(msq + RMS_EPS) * g
        rot = jnp.where(first, pltpu.roll(y, LANES - DH_B // 2, axis=1),
                        pltpu.roll(y, DH_B // 2, axis=1))
        return y * cos + rot * sin

    kc = norm_rope(proj_ref[:, OFF_KB:OFF_KB + KV_WIDTH], kg_ref[lyr, :])
    vc = proj_ref[:, OFF_VB:OFF_VB + KV_WIDTH]
    is_meta_tile = blk == 0
    for buf, val in ((kbuf_ref, kc), (ksw_ref, pltpu.roll(kc, DH_B, axis=1)),
                     (vbuf_ref, vc), (vsw_ref, pltpu.roll(vc, DH_B, axis=1))):
        val = val.astype(BF16)
        buf[KB_CUR:KB_CUR + TILE, :] = val
        for m0 in (KB_META0, KB_META1):
            buf[m0:m0 + N_META, :] = jnp.where(is_meta_tile, val[PAD_FRONT:], buf[m0:m0 + N_META, :])
    tick()

    scale = DH_B ** -0.5 * LOG2E
    qcols = []
    for j in range(HB // 2):
        qj = norm_rope(proj_ref[:, OFF_QB + j * LANES:OFF_QB + (j + 1) * LANES], qg_ref[lyr, :]) * scale
        qcols.append((jnp.where(lo_half, qj, 0.0).astype(BF16), jnp.where(lo_half, 0.0, qj).astype(BF16)))
    tick()

    grp = HB // KVH_B
    heads_same = [hh for hh in range(HB) if (hh % 2) == (hh // grp)]
    heads_swap = [hh for hh in range(HB) if (hh % 2) != (hh // grp)]
    col = lax.broadcasted_iota(jnp.int32, (1, KWIN), 1)
    big = jnp.int32(1 << 30)
    windows = ((0, 0), (KBUF_ROWS - KWIN, KWIN - 1))

    def attn_sub_block(sb):
        w0, sink_col = windows[sb]
        q0 = sb * QB
        brow = w0 + col
        in_meta0 = (brow >= KB_META0) & (brow < KB_PREV)
        in_meta1 = (brow >= KB_META1) & (brow < KB_META1 + N_META)
        in_band = (brow >= KB_PREV) & (brow < KB_META1)
        kj = jnp.where(in_meta0, PAD_FRONT + brow - KB_META0,
                       jnp.where(in_meta1, PAD_FRONT + brow - KB_META1,
                                 jnp.where(in_band, (blk - 1) * TILE + brow - KB_PREV, big)))
        qi = blk * TILE + q0 + lax.broadcasted_iota(jnp.int32, (QB, 1), 0)
        allowed = (kj <= qi) & (in_meta0 | in_meta1 | ((kj > qi - TILE) & (kj >= TILE)))

        groups = ((heads_same, kbuf_ref, vbuf_ref), (heads_swap, ksw_ref, vsw_ref))
        s_all = []
        for heads, k_ref, _ in groups:
            lhs = jnp.concatenate([qcols[hh // 2][hh % 2][q0:q0 + QB] for hh in heads], axis=0)
            s_all.append(_dot_nt(lhs, k_ref[w0:w0 + KWIN, :]))
        ps, inv_den = [[] for _ in groups], [[] for _ in groups]
        for n in range(len(heads_same)):
            for g, (heads, _, _) in enumerate(groups):
                fill = jnp.where(col == sink_col, sinks_ref[layer, heads[n]] * LOG2E, NEG_INF)
                s = jnp.where(allowed, s_all[g][n * QB:(n + 1) * QB], fill)
                pr = jnp.exp2(s - jnp.max(s, axis=-1, keepdims=True))
                inv_den[g].append(1.0 / jnp.sum(pr, axis=-1, keepdims=True))
                ps[g].append(pr.astype(BF16))
            if n == len(heads_same) // 2 - 1:
                tick()
        outs = {}
        for g, (heads, _, v_ref) in enumerate(groups):
            o_all = _dot(jnp.concatenate(ps[g], axis=0), v_ref[w0:w0 + KWIN, :])
            for n, hh in enumerate(heads):
                outs[hh] = o_all[n * QB:(n + 1) * QB] * inv_den[g][n]
        tick()
        for j in range(HB // 2):
            yb = jnp.where(lo_half, outs[2 * j], outs[2 * j + 1])
            gb = proj_ref[q0:q0 + QB, OFF_GB + j * LANES:OFF_GB + (j + 1) * LANES]
            mixed_ref[q0:q0 + QB, A_WIDTH + j * LANES:A_WIDTH + (j + 1) * LANES] = (
                yb * _silu(gb)).astype(BF16)

    for sb in range(len(windows)):
        attn_sub_block(sb)

    for buf in (kbuf_ref, ksw_ref, vbuf_ref, vsw_ref):
        buf[KB_PREV:KB_CUR, :] = buf[KB_CUR:KB_CUR + TILE, :]

    wmax = max(POOL_WINDOWS)
    ug = proj_ref[:, OFF_UC:OFF_UC + C_WIDTH] * valid
    uext_ref[wmax:wmax + TILE, :] = ug
    n_valid = jnp.maximum(row - (PAD_FRONT - 1), 0)
    for g, w in enumerate(POOL_WINDOWS):
        cs = slice(g * CG_C, (g + 1) * CG_C)
        acc = uext_ref[wmax:wmax + TILE, cs]
        for s in range(1, w):
            acc = acc + uext_ref[wmax - s:wmax - s + TILE, cs]
        cnt = (n_valid - jnp.maximum(row - w - (PAD_FRONT - 1), 0)).astype(F32)
        pooled = (acc / jnp.maximum(cnt, 1.0) - ug[:, cs]) * valid
        y = _dot(pooled.astype(BF16), pw_ref[layer, g].astype(BF16)) * ps_ref[lyr, cs]
        gc = proj_ref[:, OFF_GC + g * CG_C:OFF_GC + (g + 1) * CG_C]
        mixed_ref[:, A_WIDTH + B_WIDTH + g * CG_C:A_WIDTH + B_WIDTH + (g + 1) * CG_C] = (
            y * _silu(gc)).astype(BF16)
        tick()
    uext_ref[0:wmax, :] = ug[TILE - wmax:]


def _layer_kernel(layer, first, sinks_ref, *refs):
    n_h = 4 if first else 2
    h_refs, refs = refs[:n_h], refs[n_h:]
    (cos_ref, sin_ref, ng_ref, win_hbm, lbl_ref, qg_ref, kg_ref, hg_ref, pw_ref,
     ps_ref, wout_hbm, mstack_ref, lmask_ref, out_ref, win_ref, wout_ref, wsem, xn_ref, proja_ref,
     projb_ref, mixeda_ref, mixedb_ref, expo_ref, st_ref, kbuf_ref, ksw_ref, vbuf_ref, vsw_ref,
     uext_ref) = refs
    t = pl.program_id(0)

    @pl.when(t == 0)
    def _load_weights_and_zero_state():
        stage = (proja_ref, projb_ref)
        for w_hbm, w_vmem in ((win_hbm, win_ref), (wout_hbm, wout_ref)):
            width = w_vmem.shape[1]
            n_chunks = w_vmem.shape[0] // TILE

            def chunk_copy(k, slot, w_hbm=w_hbm, width=width):
                return pltpu.make_async_copy(w_hbm.at[layer, pl.ds(k * TILE, TILE), :],
                                             stage[slot].at[:, 0:width], wsem.at[slot])

            chunk_copy(0, 0).start()
            chunk_copy(1, 1).start()

            def pair(i, carry, w_vmem=w_vmem, width=width, n_chunks=n_chunks, chunk_copy=chunk_copy):
                for slot in (0, 1):
                    k = 2 * i + slot
                    chunk_copy(k, slot).wait()
                    w_vmem[pl.ds(pl.multiple_of(k * TILE, TILE), TILE), :] = stage[slot][:, 0:width].astype(BF16)

                    @pl.when(k + 2 < n_chunks)
                    def _next():
                        chunk_copy(k + 2, slot).start()
                return carry

            lax.fori_loop(0, n_chunks // 2, pair, 0)

        for ref in (proja_ref, projb_ref, mixeda_ref, mixedb_ref, st_ref, kbuf_ref, ksw_ref,
                    vbuf_ref, vsw_ref, uext_ref):
            ref[...] = jnp.zeros_like(ref)

    def step_rows(step_idx, rws, cols, x_ref, meta_ref=None):
        x = x_ref[rws, cols]
        if not first:
            return x
        if rws.stop is not None and rws.stop < STEP:
            return jnp.where(step_idx == 0, 0.0, x)
        n_rows = x.shape[0]
        front = jnp.concatenate([jnp.zeros((n_rows - N_META, x.shape[1]), F32), meta_ref[:, cols]], axis=0)
        return jnp.where(step_idx == 0, front, x)

    d_model = out_ref.shape[1]
    last_step = pl.num_programs(0) - 1
    rows_a, rows_b = slice(0, TILE), slice(TILE, STEP)

    def norm_rows():
        h = step_rows(t, slice(None), slice(None), *h_refs[:n_h // 2])
        ms = jnp.mean(h * h, axis=-1, keepdims=True)
        xn_ref[...] = (h * lax.rsqrt(ms + RMS_EPS) * ng_ref[layer:layer + 1, :]).astype(BF16)

    def out_chunk(rws, mixed_src, c0):
        cols = slice(c0, c0 + MM_CHUNK)
        h_res = step_rows(t - 1, rws, cols, *h_refs[n_h // 2:])
        out_ref[rws, cols] = h_res + _dot(mixed_src[...], wout_ref[:, cols])

    def in_chunk(rws, proj_dst, c0):
        cols = slice(c0, c0 + MM_CHUNK)
        proj_dst[:, cols] = _dot(xn_ref[rws, :], win_ref[:, cols])

    def out_chunks(rws, mixed_src):
        return [functools.partial(out_chunk, rws, mixed_src, c0) for c0 in range(0, d_model, MM_CHUNK)]

    def in_chunks(rws, proj_dst):
        return [functools.partial(in_chunk, rws, proj_dst, c0) for c0 in range(0, IN_COLS, MM_CHUNK)]

    def half_step(work, blk_mix, trig, proj_mix, mixed_mix):
        progress = {"ticks": 0, "done": 0}

        def tick():
            progress["ticks"] += 1
            target = min(len(work), -(-len(work) * progress["ticks"] // (MIX_TICKS + 1)))
            while progress["done"] < target:
                work[progress["done"]]()
                progress["done"] += 1

        tick()
        _mix_tile(layer, blk_mix, sinks_ref, cos_ref, sin_ref, trig, lbl_ref, qg_ref, kg_ref, hg_ref,
                  pw_ref, ps_ref, mstack_ref, lmask_ref, proj_mix, mixed_mix, expo_ref, st_ref,
                  kbuf_ref, ksw_ref, vbuf_ref, vsw_ref, uext_ref, tick)
        assert progress["done"] == len(work), "MIX_TICKS must match the tick() calls in _mix_tile"

    trig_b, trig_a = rows_a, rows_b

    @pl.when(t == 0)
    def _first_step():
        norm_rows()
        for chunk in in_chunks(rows_b, projb_ref):
            chunk()

    @pl.when((t > 0) & (t < last_step))
    def _main_step():
        norm_rows()
        half_step(out_chunks(rows_a, mixeda_ref) + in_chunks(rows_a, proja_ref), 2 * t - 2, trig_b,
                  projb_ref, mixedb_ref)
        half_step(out_chunks(rows_b, mixedb_ref) + in_chunks(rows_b, projb_ref), 2 * t - 1, trig_a,
                  proja_ref, mixeda_ref)

    @pl.when(t == last_step)
    def _drain_step():
        half_step(out_chunks(rows_a, mixeda_ref), 2 * t - 2, trig_b, projb_ref, mixedb_ref)
        for chunk in out_chunks(rows_b, mixedb_ref):
            chunk()


def _layer_call(layer, first, last, h_in, meta, cos, sin, sinks, norm_g, w_in, lb_logits, qg, kg, hg,
                pool_w, pool_scale, w_out, mstack, lmask):
    d = h_in.shape[1]
    n_steps = cos.shape[0] // STEP - 1

    def rows(width, lag, skip_front=False):
        off = lag + (1 if skip_front else 0)
        hi = n_steps - 1 - (1 if skip_front else 0)
        return pl.BlockSpec((STEP, width), lambda t: (jnp.clip(t - off, 0, hi), 0))

    def whole(shape, single=False):
        idx = lambda t: (0,) * len(shape)
        if single:
            return pl.BlockSpec(shape, idx, pipeline_mode=pl.Buffered(1))
        return pl.BlockSpec(shape, idx)

    in_specs = [pl.BlockSpec(memory_space=pltpu.SMEM)]
    args = [sinks]
    for lag in (0, 1):
        in_specs.append(rows(d, lag, skip_front=first))
        args.append(h_in)
        if first:
            in_specs.append(whole(meta.shape))
            args.append(meta)
    in_specs += [
        pl.BlockSpec((STEP, LANES), lambda t: (t, 0)),
        pl.BlockSpec((STEP, LANES), lambda t: (t, 0)),
        whole(norm_g.shape),
        pl.BlockSpec(memory_space=pl.ANY),
        whole(lb_logits.shape),
        whole(qg.shape), whole(kg.shape), whole(hg.shape),
        whole(pool_w.shape), whole(pool_scale.shape),
        pl.BlockSpec(memory_space=pl.ANY),
        whole(mstack.shape, single=True), whole(lmask.shape, single=True),
    ]
    args += [cos, sin, norm_g, w_in, lb_logits, qg, kg, hg, pool_w, pool_scale, w_out, mstack, lmask]
    out_rows = (n_steps - 1) * STEP if last else n_steps * STEP
    assert w_in.shape[1] % (2 * TILE) == 0 and w_out.shape[1] % (2 * TILE) == 0
    assert w_out.shape[2] <= IN_COLS
    scratch = [
        pltpu.VMEM(w_in.shape[1:], BF16),
        pltpu.VMEM(w_out.shape[1:], BF16),
        pltpu.SemaphoreType.DMA((2,)),
        pltpu.VMEM((STEP, d), BF16),
        pltpu.VMEM((TILE, IN_COLS), F32),
        pltpu.VMEM((TILE, IN_COLS), F32),
        pltpu.VMEM((TILE, MIX_WIDTH), BF16),
        pltpu.VMEM((TILE, MIX_WIDTH), BF16),
        pltpu.VMEM((EXPO_BLOCKS * TILE, A_WIDTH), F32),
        pltpu.VMEM((HA, DK_A, DK_A), F32),
        pltpu.VMEM((KBUF_ROWS, LANES), BF16),
        pltpu.VMEM((KBUF_ROWS, LANES), BF16),
        pltpu.VMEM((KBUF_ROWS, LANES), BF16),
        pltpu.VMEM((KBUF_ROWS, LANES), BF16),
        pltpu.VMEM((max(POOL_WINDOWS) + TILE, C_WIDTH), F32),
    ]
    return pl.pallas_call(
        functools.partial(_layer_kernel, layer, first),
        grid=(n_steps + 1,),
        in_specs=in_specs,
        out_specs=rows(d, 1, skip_front=last),
        out_shape=jax.ShapeDtypeStruct((out_rows, d), F32),
        scratch_shapes=scratch,
        compiler_params=pltpu.CompilerParams(
            dimension_semantics=("arbitrary",), vmem_limit_bytes=VMEM_LIMIT),
        name=f"hybrid_layer{layer}",
    )(*args)


def kernel(x, meta_tokens, lb_logits, norm_g, w_in, q_norm_g, k_norm_g, attn_sinks, hgrn_norm_g,
           pool_w, pool_scale, w_out):
    b, seq, d = x.shape
    depth = w_in.shape[0]
    assert b == 1 and seq % STEP == 0
    assert w_in.shape[2] == IN_COLS and w_out.shape[1] == MIX_WIDTH
    assert meta_tokens.shape[0] == N_META

    p = 2 * STEP + seq
    pos = np.arange(p, dtype=np.float64) - TILE - (STEP - N_META)
    half = DH_B // 2
    inv = np.power(ROPE_THETA, -np.arange(half, dtype=np.float64) * 2.0 / DH_B)
    ang = pos[:, None] * inv[None, :]
    cos = jnp.asarray(np.tile(np.cos(ang), (1, LANES // half)), F32)
    sin = jnp.asarray(np.tile(np.concatenate([-np.sin(ang), np.sin(ang)], axis=1), (1, LANES // DH_B)), F32)

    mstack_np, lmask_np = _decay_constants()
    mstack = jnp.asarray(mstack_np, BF16)
    lmask = jnp.asarray(lmask_np, F32)

    qg = jnp.tile(q_norm_g, (1, LANES // DH_B))
    kg = jnp.tile(k_norm_g, (1, LANES // DH_B))
    h = x[0]
    for l in range(depth):
        h = _layer_call(l, l == 0, l == depth - 1, h, meta_tokens.astype(F32), cos, sin, attn_sinks,
                        norm_g, w_in, lb_logits, qg, kg, hgrn_norm_g, pool_w, pool_scale, w_out,
                        mstack, lmask)
    return h[None]
```

```python
import functools

import numpy as np
import jax
import jax.numpy as jnp
from jax import lax
from jax.experimental import pallas as pl
from jax.experimental.pallas import tpu as pltpu

F32 = jnp.float32
BF16 = jnp.bfloat16

N_META = 16
TILE = 128
PAD_FRONT = TILE - N_META
STEP = 2 * TILE
RMS_EPS = 1e-6
NEG_INF = -1e30
LOG_FLOOR = 1e-30
ROPE_THETA = 10000.0
LOG2E = 1.4426950408889634

HA, DK_A = 4, 128
HGRN_GROUP = 2
A_WIDTH = HA * DK_A
HB, KVH_B, DH_B = 16, 2, 64
B_WIDTH = HB * DH_B
KV_WIDTH = KVH_B * DH_B
POOL_WINDOWS = (2, 4, 8, 16)
CG_C = 128
C_WIDTH = len(POOL_WINDOWS) * CG_C
MIX_WIDTH = A_WIDTH + B_WIDTH + C_WIDTH

OFF_QA = 0
OFF_FA = OFF_QA + A_WIDTH
OFF_IA = OFF_FA + A_WIDTH
OFF_GA = OFF_IA + A_WIDTH
OFF_QB = OFF_GA + A_WIDTH
OFF_KB = OFF_QB + B_WIDTH
OFF_VB = OFF_KB + KV_WIDTH
OFF_GB = OFF_VB + KV_WIDTH
OFF_UC = OFF_GB + B_WIDTH
OFF_GC = OFF_UC + C_WIDTH
IN_COLS = OFF_GC + C_WIDTH
MM_CHUNK = 256

LANES = 128
QB = 64
KWIN = 256
KB_META0 = KWIN - N_META - TILE - QB
KB_PREV = KB_META0 + N_META
KB_CUR = KB_PREV + TILE
KB_META1 = KB_CUR + TILE
KBUF_ROWS = KB_PREV + QB + KWIN
assert KB_META0 >= 16 and KBUF_ROWS - KB_META1 - N_META >= 16
LEVELS = (128, 64, 32, 16, 8, 4, 2)
SUBLANES = 8
EXPO_BLOCKS = 2
MIX_TICKS = 1 + HA + 2 + 2 * (TILE // QB) + len(POOL_WINDOWS)
V7X_VMEM_BYTES = 64 * 1024 * 1024
VMEM_LIMIT = V7X_VMEM_BYTES - 4 * 1024 * 1024


def _decay_constants():
    r = np.arange(TILE)
    masks = []
    for blk in LEVELS:
        half = blk // 2
        pos = r % blk
        same = (r[:, None] // blk) == (r[None, :] // blk)
        masks.append((same & (pos[:, None] >= half) & (pos[None, :] < half)).astype(np.float32))
    tril = np.tril(np.ones((TILE, TILE), np.float32))
    return np.concatenate([tril, tril], axis=1), np.stack(masks, axis=0)


def _dot(a, b):
    return jnp.dot(a, b, preferred_element_type=F32)


def _dot_nt(a, b):
    return lax.dot_general(a, b, (((1,), (1,)), ((), ())), preferred_element_type=F32)


def _dot_tn(a, b):
    return lax.dot_general(a, b, (((0,), (0,)), ((), ())), preferred_element_type=F32)


def _sigmoid(x):
    return 0.5 * jnp.tanh(0.5 * x) + 0.5


def _silu(x):
    t = 0.5 * x
    return t * jnp.tanh(t) + t


def _mix_tile(layer, blk, sinks_ref, cos_ref, sin_ref, trig_rows, lbl_ref, qg_ref, kg_ref, hg_ref,
              pw_ref, ps_ref, mstack_ref, lmask_ref, proj_ref, mixed_ref, expo_ref, st_ref,
              kbuf_ref, ksw_ref, vbuf_ref, vsw_ref, uext_ref, score_ref, tick):
    lyr = slice(layer, layer + 1)
    rloc = lax.broadcasted_iota(jnp.int32, (TILE, 1), 0)
    row = blk * TILE + rloc
    valid = (row >= PAD_FRONT).astype(F32)

    rows = [lbl_ref[j:j + 1, :] for j in range(lbl_ref.shape[0])]
    mx = functools.reduce(jnp.maximum, rows)
    es = [jnp.exp(x - mx) for x in rows]
    lb = sum(es[1:layer + 1], jnp.zeros_like(mx)) / sum(es)

    z = proj_ref[:, OFF_FA:OFF_FA + A_WIDTH]
    sig = _sigmoid(z)
    f = lb + (1.0 - lb) * sig
    logf = jnp.log2(jnp.maximum(f, LOG_FLOOR)) * valid
    kk = (1.0 - lb) * (1.0 - sig) * valid
    proj_ref[:, OFF_QA:OFF_QA + A_WIDTH] = _silu(proj_ref[:, OFF_QA:OFF_QA + A_WIDTH])
    proj_ref[:, OFF_FA:OFF_FA + A_WIDTH] = kk
    lf_hi = logf.astype(BF16)
    lf_lo = (logf - lf_hi.astype(F32)).astype(BF16)
    expo_ref[0:TILE, :] = _dot(mstack_ref[...], jnp.concatenate([lf_hi, lf_lo], axis=0))
    expo_ref[TILE:2 * TILE, :] = jnp.where(valid > 0.0, jnp.maximum(f, LOG_FLOOR), 1.0)
    tick()

    def level_factors(bsz, g, f_row):
        half = bsz // 2
        if bsz == 2:
            return jnp.where((rloc & 1) == 1, f_row, 1.0)
        if bsz <= SUBLANES:
            sub = lax.broadcasted_iota(jnp.int32, (SUBLANES, 1), 0)
            vregs = []
            for r0 in range(0, TILE, SUBLANES):
                gv = g[r0:r0 + SUBLANES]
                piv = None
                for b0 in range(0, SUBLANES, bsz):
                    row = jnp.broadcast_to(gv[b0 + half - 1:b0 + half, :], gv.shape)
                    piv = row if piv is None else jnp.where(sub >= b0, row, piv)
                vregs.append(gv - piv)
            d = jnp.concatenate(vregs, axis=0)
            return jnp.exp2(jnp.where((rloc & (bsz - 1)) >= half, d, -d))
        parts = []
        for b0 in range(0, TILE, bsz):
            piv = jnp.broadcast_to(g[b0 + half - 1:b0 + half, :], (half, g.shape[1]))
            parts += [piv - g[b0:b0 + half], g[b0 + half:b0 + bsz] - piv]
        return jnp.exp2(jnp.concatenate(parts, axis=0))

    def hgrn_heads(hds):
        cs = {hd: slice(hd * DK_A, (hd + 1) * DK_A) for hd in hds}
        qf_h = {hd: proj_ref[:, OFF_QA + hd * DK_A:OFF_QA + (hd + 1) * DK_A] for hd in hds}
        kk_h = {hd: proj_ref[:, OFF_FA + hd * DK_A:OFF_FA + (hd + 1) * DK_A] for hd in hds}
        diag = {hd: jnp.sum(qf_h[hd] * kk_h[hd], axis=-1, keepdims=True) for hd in hds}
        attn = {}
        g_inc = {hd: expo_ref[0:TILE, cs[hd]] for hd in hds}
        f_row = {hd: expo_ref[TILE:2 * TILE, cs[hd]] for hd in hds}
        for li, bsz in enumerate(LEVELS):
            upper = (rloc & (bsz - 1)) >= (bsz // 2)
            for hd in hds:
                xb = (jnp.where(upper, qf_h[hd], kk_h[hd])
                      * level_factors(bsz, g_inc[hd], f_row[hd])).astype(BF16)
                term = _dot_nt(xb, xb) * lmask_ref[li]
                attn[hd] = term if li == 0 else attn[hd] + term
            if li == len(LEVELS) // 2:
                tick()
        v_f32 = {hd: proj_ref[:, OFF_IA + hd * DK_A:OFF_IA + (hd + 1) * DK_A] for hd in hds}
        v_h = {hd: v_f32[hd].astype(BF16) for hd in hds}
        st = {hd: st_ref[hd] for hd in hds}
        o = {hd: _dot(attn[hd].astype(BF16), v_h[hd]) + diag[hd] * v_f32[hd] for hd in hds}
        for hd in hds:
            q_dec = (qf_h[hd] * jnp.exp2(g_inc[hd])).astype(BF16)
            o[hd] = o[hd] + _dot_nt(q_dec, st[hd].astype(BF16))
        for hd in hds:
            g_tot = g_inc[hd][TILE - 1:TILE]
            k_dec = (kk_h[hd] * jnp.exp2(g_tot - g_inc[hd])).astype(BF16)
            st_ref[hd] = jnp.exp2(g_tot) * st[hd] + _dot_tn(v_h[hd], k_dec)
        for _ in range(len(hds) - 1):
            tick()
        for hd in hds:
            ya = o[hd] * lax.rsqrt(jnp.mean(o[hd] * o[hd], axis=-1, keepdims=True) + RMS_EPS) * hg_ref[lyr, :]
            ga = proj_ref[:, OFF_GA + hd * DK_A:OFF_GA + (hd + 1) * DK_A]
            mixed_ref[:, hd * DK_A:(hd + 1) * DK_A] = (ya * _silu(ga)).astype(BF16)

    for hd0 in range(0, HA, HGRN_GROUP):
        hgrn_heads(tuple(range(hd0, hd0 + HGRN_GROUP)))

    lane = lax.broadcasted_iota(jnp.int32, (1, LANES), 1)
    lo_half = lane < DH_B
    first = (lane & (DH_B - 1)) < DH_B // 2
    cos = cos_ref[trig_rows, :]
    sin = sin_ref[trig_rows, :]

    def norm_rope(x, g):
        sq = x * x
        s0 = jnp.sum(jnp.where(lo_half, sq, 0.0), axis=-1, keepdims=True)
        s1 = jnp.sum(jnp.where(lo_half, 0.0, sq), axis=-1, keepdims=True)
        msq = jnp.where(lo_half, s0, s1) * (1.0 / DH_B)
        y = x * lax.rsqrt(msq + RMS_EPS) * g
        rot = jnp.where(first, pltpu.roll(y, LANES - DH_B // 2, axis=1),
                        pltpu.roll(y, DH_B // 2, axis=1))
        return y * cos + rot * sin

    kc = norm_rope(proj_ref[:, OFF_KB:OFF_KB + KV_WIDTH], kg_ref[lyr, :])
    vc = proj_ref[:, OFF_VB:OFF_VB + KV_WIDTH]
    is_meta_tile = blk == 0
    for buf, val in ((kbuf_ref, kc), (ksw_ref, pltpu.roll(kc, DH_B, axis=1)),
                     (vbuf_ref, vc), (vsw_ref, pltpu.roll(vc, DH_B, axis=1))):
        val = val.astype(BF16)
        buf[KB_CUR:KB_CUR + TILE, :] = val
        for m0 in (KB_META0, KB_META1):
            buf[m0:m0 + N_META, :] = jnp.where(is_meta_tile, val[PAD_FRONT:], buf[m0:m0 + N_META, :])
    tick()

    scale = DH_B ** -0.5 * LOG2E
    qcols = []
    for j in range(HB // 2):
        qj = norm_rope(proj_ref[:, OFF_QB + j * LANES:OFF_QB + (j + 1) * LANES], qg_ref[lyr, :]) * scale
        qcols.append((jnp.where(lo_half, qj, 0.0).astype(BF16), jnp.where(lo_half, 0.0, qj).astype(BF16)))
    tick()

    grp = HB // KVH_B
    heads_same = [hh for hh in range(HB) if (hh % 2) == (hh // grp)]
    heads_swap = [hh for hh in range(HB) if (hh % 2) != (hh // grp)]
    col = lax.broadcasted_iota(jnp.int32, (1, KWIN), 1)
    big = jnp.int32(1 << 30)
    windows = ((0, 0), (KBUF_ROWS - KWIN, KWIN - 1))

    def attn_sub_block(sb):
        w0, sink_col = windows[sb]
        q0 = sb * QB
        brow = w0 + col
        in_meta0 = (brow >= KB_META0) & (brow < KB_PREV)
        in_meta1 = (brow >= KB_META1) & (brow < KB_META1 + N_META)
        in_band = (brow >= KB_PREV) & (brow < KB_META1)
        kj = jnp.where(in_meta0, PAD_FRONT + brow - KB_META0,
                       jnp.where(in_meta1, PAD_FRONT + brow - KB_META1,
                                 jnp.where(in_band, (blk - 1) * TILE + brow - KB_PREV, big)))
        qi = blk * TILE + q0 + lax.broadcasted_iota(jnp.int32, (QB, 1), 0)
        allowed = (kj <= qi) & (in_meta0 | in_meta1 | ((kj > qi - TILE) & (kj >= TILE)))

        groups = ((heads_same, kbuf_ref, vbuf_ref), (heads_swap, ksw_ref, vsw_ref))
        for g, (heads, k_ref, _) in enumerate(groups):
            lhs = jnp.concatenate([qcols[hh // 2][hh % 2][q0:q0 + QB] for hh in heads], axis=0)
            score_ref[g] = _dot_nt(lhs, k_ref[w0:w0 + KWIN, :])
        s_all = [score_ref.at[g] for g in range(len(groups))]
        ps, inv_den = [[] for _ in groups], [[] for _ in groups]
        for n in range(len(heads_same)):
            for g, (heads, _, _) in enumerate(groups):
                fill = jnp.where(col == sink_col, sinks_ref[layer, heads[n]] * LOG2E, NEG_INF)
                s = jnp.where(allowed, s_all[g][n * QB:(n + 1) * QB], fill)
                pr = jnp.exp2(s - jnp.max(s, axis=-1, keepdims=True))
                inv_den[g].append(1.0 / jnp.sum(pr, axis=-1, keepdims=True))
                ps[g].append(pr.astype(BF16))
            if n == len(heads_same) // 2 - 1:
                tick()
        outs = {}
        for g, (heads, _, v_ref) in enumerate(groups):
            o_all = _dot(jnp.concatenate(ps[g], axis=0), v_ref[w0:w0 + KWIN, :])
            for n, hh in enumerate(heads):
                outs[hh] = o_all[n * QB:(n + 1) * QB] * inv_den[g][n]
        tick()
        for j in range(HB // 2):
            yb = jnp.where(lo_half, outs[2 * j], outs[2 * j + 1])
            gb = proj_ref[q0:q0 + QB, OFF_GB + j * LANES:OFF_GB + (j + 1) * LANES]
            mixed_ref[q0:q0 + QB, A_WIDTH + j * LANES:A_WIDTH + (j + 1) * LANES] = (
                yb * _silu(gb)).astype(BF16)

    for sb in range(len(windows)):
        attn_sub_block(sb)

    for buf in (kbuf_ref, ksw_ref, vbuf_ref, vsw_ref):
        buf[KB_PREV:KB_CUR, :] = buf[KB_CUR:KB_CUR + TILE, :]

    wmax = max(POOL_WINDOWS)
    ug = proj_ref[:, OFF_UC:OFF_UC + C_WIDTH] * valid
    uext_ref[wmax:wmax + TILE, :] = ug
    n_valid = jnp.maximum(row - (PAD_FRONT - 1), 0)
    for g, w in enumerate(POOL_WINDOWS):
        cs = slice(g * CG_C, (g + 1) * CG_C)
        acc = uext_ref[wmax:wmax + TILE, cs]
        for s in range(1, w):
            acc = acc + uext_ref[wmax - s:wmax - s + TILE, cs]
        cnt = (n_valid - jnp.maximum(row - w - (PAD_FRONT - 1), 0)).astype(F32)
        pooled = (acc / jnp.maximum(cnt, 1.0) - ug[:, cs]) * valid
        y = _dot(pooled.astype(BF16), pw_ref[layer, g].astype(BF16)) * ps_ref[lyr, cs]
        gc = proj_ref[:, OFF_GC + g * CG_C:OFF_GC + (g + 1) * CG_C]
        mixed_ref[:, A_WIDTH + B_WIDTH + g * CG_C:A_WIDTH + B_WIDTH + (g + 1) * CG_C] = (
            y * _silu(gc)).astype(BF16)
        tick()
    uext_ref[0:wmax, :] = ug[TILE - wmax:]


def _layer_kernel(layer, first, sinks_ref, *refs):
    n_h = 4 if first else 2
    h_refs, refs = refs[:n_h], refs[n_h:]
    (cos_ref, sin_ref, ng_ref, win_hbm, lbl_ref, qg_ref, kg_ref, hg_ref, pw_ref,
     ps_ref, wout_hbm, mstack_ref, lmask_ref, out_ref, win_ref, wout_ref, wsem, xn_ref, proja_ref,
     projb_ref, mixeda_ref, mixedb_ref, expo_ref, st_ref, kbuf_ref, ksw_ref, vbuf_ref, vsw_ref,
     uext_ref, score_ref) = refs
    t = pl.program_id(0)

    @pl.when(t == 0)
    def _load_weights_and_zero_state():
        stage = (proja_ref, projb_ref)
        for w_hbm, w_vmem in ((win_hbm, win_ref), (wout_hbm, wout_ref)):
            width = w_vmem.shape[1]
            n_chunks = w_vmem.shape[0] // TILE

            def chunk_copy(k, slot, w_hbm=w_hbm, width=width):
                return pltpu.make_async_copy(w_hbm.at[layer, pl.ds(k * TILE, TILE), :],
                                             stage[slot].at[:, 0:width], wsem.at[slot])

            chunk_copy(0, 0).start()
            chunk_copy(1, 1).start()

            def pair(i, carry, w_vmem=w_vmem, width=width, n_chunks=n_chunks, chunk_copy=chunk_copy):
                for slot in (0, 1):
                    k = 2 * i + slot
                    chunk_copy(k, slot).wait()
                    w_vmem[pl.ds(pl.multiple_of(k * TILE, TILE), TILE), :] = stage[slot][:, 0:width].astype(BF16)

                    @pl.when(k + 2 < n_chunks)
                    def _next():
                        chunk_copy(k + 2, slot).start()
                return carry

            lax.fori_loop(0, n_chunks // 2, pair, 0)

        for ref in (proja_ref, projb_ref, mixeda_ref, mixedb_ref, st_ref, kbuf_ref, ksw_ref,
                    vbuf_ref, vsw_ref, uext_ref):
            ref[...] = jnp.zeros_like(ref)

    def step_rows(step_idx, rws, cols, x_ref, meta_ref=None):
        x = x_ref[rws, cols]
        if not first:
            return x
        if rws.stop is not None and rws.stop < STEP:
            return jnp.where(step_idx == 0, 0.0, x)
        n_rows = x.shape[0]
        front = jnp.concatenate([jnp.zeros((n_rows - N_META, x.shape[1]), F32), meta_ref[:, cols]], axis=0)
        return jnp.where(step_idx == 0, front, x)

    d_model = out_ref.shape[1]
    last_step = pl.num_programs(0) - 1
    rows_a, rows_b = slice(0, TILE), slice(TILE, STEP)

    def norm_rows():
        h = step_rows(t, slice(None), slice(None), *h_refs[:n_h // 2])
        ms = jnp.mean(h * h, axis=-1, keepdims=True)
        xn_ref[...] = (h * lax.rsqrt(ms + RMS_EPS) * ng_ref[layer:layer + 1, :]).astype(BF16)

    def out_chunk(rws, mixed_src, c0):
        cols = slice(c0, c0 + MM_CHUNK)
        h_res = step_rows(t - 1, rws, cols, *h_refs[n_h // 2:])
        out_ref[rws, cols] = h_res + _dot(mixed_src[...], wout_ref[:, cols])

    def in_chunk(rws, proj_dst, c0):
        cols = slice(c0, c0 + MM_CHUNK)
        proj_dst[:, cols] = _dot(xn_ref[rws, :], win_ref[:, cols])

    def out_chunks(rws, mixed_src):
        return [functools.partial(out_chunk, rws, mixed_src, c0) for c0 in range(0, d_model, MM_CHUNK)]

    def in_chunks(rws, proj_dst):
        return [functools.partial(in_chunk, rws, proj_dst, c0) for c0 in range(0, IN_COLS, MM_CHUNK)]

    def half_step(work, blk_mix, trig, proj_mix, mixed_mix):
        progress = {"ticks": 0, "done": 0}

        def tick():
            progress["ticks"] += 1
            target = min(len(work), -(-len(work) * progress["ticks"] // (MIX_TICKS + 1)))
            while progress["done"] < target:
                work[progress["done"]]()
                progress["done"] += 1

        tick()
        _mix_tile(layer, blk_mix, sinks_ref, cos_ref, sin_ref, trig, lbl_ref, qg_ref, kg_ref, hg_ref,
                  pw_ref, ps_ref, mstack_ref, lmask_ref, proj_mix, mixed_mix, expo_ref, st_ref,
                  kbuf_ref, ksw_ref, vbuf_ref, vsw_ref, uext_ref, score_ref, tick)
        assert progress["done"] == len(work), "MIX_TICKS must match the tick() calls in _mix_tile"

    trig_b, trig_a = rows_a, rows_b

    @pl.when(t == 0)
    def _first_step():
        norm_rows()
        for chunk in in_chunks(rows_b, projb_ref):
            chunk()

    @pl.when((t > 0) & (t < last_step))
    def _main_step():
        norm_rows()
        half_step(out_chunks(rows_a, mixeda_ref) + in_chunks(rows_a, proja_ref), 2 * t - 2, trig_b,
                  projb_ref, mixedb_ref)
        half_step(out_chunks(rows_b, mixedb_ref) + in_chunks(rows_b, projb_ref), 2 * t - 1, trig_a,
                  proja_ref, mixeda_ref)

    @pl.when(t == last_step)
    def _drain_step():
        half_step(out_chunks(rows_a, mixeda_ref), 2 * t - 2, trig_b, projb_ref, mixedb_ref)
        for chunk in out_chunks(rows_b, mixedb_ref):
            chunk()


def _layer_call(layer, first, last, h_in, meta, cos, sin, sinks, norm_g, w_in, lb_logits, qg, kg, hg,
                pool_w, pool_scale, w_out, mstack, lmask):
    d = h_in.shape[1]
    n_steps = cos.shape[0] // STEP - 1

    def rows(width, lag, skip_front=False):
        off = lag + (1 if skip_front else 0)
        hi = n_steps - 1 - (1 if skip_front else 0)
        return pl.BlockSpec((STEP, width), lambda t: (jnp.clip(t - off, 0, hi), 0))

    def whole(shape, single=False):
        idx = lambda t: (0,) * len(shape)
        if single:
            return pl.BlockSpec(shape, idx, pipeline_mode=pl.Buffered(1))
        return pl.BlockSpec(shape, idx)

    in_specs = [pl.BlockSpec(memory_space=pltpu.SMEM)]
    args = [sinks]
    for lag in (0, 1):
        in_specs.append(rows(d, lag, skip_front=first))
        args.append(h_in)
        if first:
            in_specs.append(whole(meta.shape))
            args.append(meta)
    in_specs += [
        pl.BlockSpec((STEP, LANES), lambda t: (t, 0)),
        pl.BlockSpec((STEP, LANES), lambda t: (t, 0)),
        whole(norm_g.shape),
        pl.BlockSpec(memory_space=pl.ANY),
        whole(lb_logits.shape),
        whole(qg.shape), whole(kg.shape), whole(hg.shape),
        whole(pool_w.shape), whole(pool_scale.shape),
        pl.BlockSpec(memory_space=pl.ANY),
        whole(mstack.shape, single=True), whole(lmask.shape, single=True),
    ]
    args += [cos, sin, norm_g, w_in, lb_logits, qg, kg, hg, pool_w, pool_scale, w_out, mstack, lmask]
    out_rows = (n_steps - 1) * STEP if last else n_steps * STEP
    assert w_in.shape[1] % (2 * TILE) == 0 and w_out.shape[1] % (2 * TILE) == 0
    assert w_out.shape[2] <= IN_COLS
    scratch = [
        pltpu.VMEM(w_in.shape[1:], BF16),
        pltpu.VMEM(w_out.shape[1:], BF16),
        pltpu.SemaphoreType.DMA((2,)),
        pltpu.VMEM((STEP, d), BF16),
        pltpu.VMEM((TILE, IN_COLS), F32),
        pltpu.VMEM((TILE, IN_COLS), F32),
        pltpu.VMEM((TILE, MIX_WIDTH), BF16),
        pltpu.VMEM((TILE, MIX_WIDTH), BF16),
        pltpu.VMEM((EXPO_BLOCKS * TILE, A_WIDTH), F32),
        pltpu.VMEM((HA, DK_A, DK_A), F32),
        pltpu.VMEM((KBUF_ROWS, LANES), BF16),
        pltpu.VMEM((KBUF_ROWS, LANES), BF16),
        pltpu.VMEM((KBUF_ROWS, LANES), BF16),
        pltpu.VMEM((KBUF_ROWS, LANES), BF16),
        pltpu.VMEM((max(POOL_WINDOWS) + TILE, C_WIDTH), F32),
        pltpu.VMEM((KVH_B, (HB // KVH_B) * QB, KWIN), F32),
    ]
    return pl.pallas_call(
        functools.partial(_layer_kernel, layer, first),
        grid=(n_steps + 1,),
        in_specs=in_specs,
        out_specs=rows(d, 1, skip_front=last),
        out_shape=jax.ShapeDtypeStruct((out_rows, d), F32),
        scratch_shapes=scratch,
        compiler_params=pltpu.CompilerParams(
            dimension_semantics=("arbitrary",), vmem_limit_bytes=VMEM_LIMIT),
        name=f"hybrid_layer{layer}",
    )(*args)


def kernel(x, meta_tokens, lb_logits, norm_g, w_in, q_norm_g, k_norm_g, attn_sinks, hgrn_norm_g,
           pool_w, pool_scale, w_out):
    b, seq, d = x.shape
    depth = w_in.shape[0]
    assert b == 1 and seq % STEP == 0
    assert w_in.shape[2] == IN_COLS and w_out.shape[1] == MIX_WIDTH
    assert meta_tokens.shape[0] == N_META

    p = 2 * STEP + seq
    pos = np.arange(p, dtype=np.float64) - TILE - (STEP - N_META)
    half = DH_B // 2
    inv = np.power(ROPE_THETA, -np.arange(half, dtype=np.float64) * 2.0 / DH_B)
    ang = pos[:, None] * inv[None, :]
    cos = jnp.asarray(np.tile(np.cos(ang), (1, LANES // half)), F32)
    sin = jnp.asarray(np.tile(np.concatenate([-np.sin(ang), np.sin(ang)], axis=1), (1, LANES // DH_B)), F32)

    mstack_np, lmask_np = _decay_constants()
    mstack = jnp.asarray(mstack_np, BF16)
    lmask = jnp.asarray(lmask_np, F32)

    qg = jnp.tile(q_norm_g, (1, LANES // DH_B))
    kg = jnp.tile(k_norm_g, (1, LANES // DH_B))
    h = x[0]
    for l in range(depth):
        h = _layer_call(l, l == 0, l == depth - 1, h, meta_tokens.astype(F32), cos, sin, attn_sinks,
                        norm_g, w_in, lb_logits, qg, kg, hgrn_norm_g, pool_w, pool_scale, w_out,
                        mstack, lmask)
    return h[None]
```

```python
import functools

import numpy as np
import jax
import jax.numpy as jnp
from jax import lax
from jax.experimental import pallas as pl
from jax.experimental.pallas import tpu as pltpu

F32 = jnp.float32
BF16 = jnp.bfloat16

N_META = 16
TILE = 128
PAD_FRONT = TILE - N_META
STEP = 2 * TILE
RMS_EPS = 1e-6
NEG_INF = -1e30
LOG_FLOOR = 1e-30
ROPE_THETA = 10000.0
LOG2E = 1.4426950408889634

HA, DK_A = 4, 128
HGRN_GROUP = 2
A_WIDTH = HA * DK_A
HB, KVH_B, DH_B = 16, 2, 64
B_WIDTH = HB * DH_B
KV_WIDTH = KVH_B * DH_B
POOL_WINDOWS = (2, 4, 8, 16)
CG_C = 128
C_WIDTH = len(POOL_WINDOWS) * CG_C
MIX_WIDTH = A_WIDTH + B_WIDTH + C_WIDTH

OFF_QA = 0
OFF_FA = OFF_QA + A_WIDTH
OFF_IA = OFF_FA + A_WIDTH
OFF_GA = OFF_IA + A_WIDTH
OFF_QB = OFF_GA + A_WIDTH
OFF_KB = OFF_QB + B_WIDTH
OFF_VB = OFF_KB + KV_WIDTH
OFF_GB = OFF_VB + KV_WIDTH
OFF_UC = OFF_GB + B_WIDTH
OFF_GC = OFF_UC + C_WIDTH
IN_COLS = OFF_GC + C_WIDTH
MM_CHUNK = 256

LANES = 128
QB = 64
KWIN = 256
KB_META0 = KWIN - N_META - TILE - QB
KB_PREV = KB_META0 + N_META
KB_CUR = KB_PREV + TILE
KB_META1 = KB_CUR + TILE
KBUF_ROWS = KB_PREV + QB + KWIN
assert KB_META0 >= 16 and KBUF_ROWS - KB_META1 - N_META >= 16
LEVELS = (128, 64, 32, 16, 8, 4, 2)
SUBLANES = 8
EXPO_BLOCKS = 2
MIX_TICKS = 1 + HA + 2 + 2 * (TILE // QB) + len(POOL_WINDOWS)
V7X_VMEM_BYTES = 64 * 1024 * 1024
VMEM_LIMIT = V7X_VMEM_BYTES - 4 * 1024 * 1024


def _decay_constants():
    r = np.arange(TILE)
    masks = []
    for blk in LEVELS:
        half = blk // 2
        pos = r % blk
        same = (r[:, None] // blk) == (r[None, :] // blk)
        masks.append((same & (pos[:, None] >= half) & (pos[None, :] < half)).astype(np.float32))
    tril = np.tril(np.ones((TILE, TILE), np.float32))
    return np.concatenate([tril, tril], axis=1), np.stack(masks, axis=0)


def _dot(a, b):
    return jnp.dot(a, b, preferred_element_type=F32)


def _dot_nt(a, b):
    return lax.dot_general(a, b, (((1,), (1,)), ((), ())), preferred_element_type=F32)


def _dot_tn(a, b):
    return lax.dot_general(a, b, (((0,), (0,)), ((), ())), preferred_element_type=F32)


def _sigmoid(x):
    return 0.5 * jnp.tanh(0.5 * x) + 0.5


def _silu(x):
    t = 0.5 * x
    return t * jnp.tanh(t) + t


def _mix_tile(layer, blk, sinks_ref, cos_ref, sin_ref, trig_rows, lbl_ref, qg_ref, kg_ref, hg_ref,
              pw_ref, ps_ref, mstack_ref, lmask_ref, proj_ref, mixed_ref, expo_ref, st_ref,
              kbuf_ref, ksw_ref, vbuf_ref, vsw_ref, uext_ref, score_ref, pv_ref, tick):
    lyr = slice(layer, layer + 1)
    rloc = lax.broadcasted_iota(jnp.int32, (TILE, 1), 0)
    row = blk * TILE + rloc
    valid = (row >= PAD_FRONT).astype(F32)

    rows = [lbl_ref[j:j + 1, :] for j in range(lbl_ref.shape[0])]
    mx = functools.reduce(jnp.maximum, rows)
    es = [jnp.exp(x - mx) for x in rows]
    lb = sum(es[1:layer + 1], jnp.zeros_like(mx)) / sum(es)

    z = proj_ref[:, OFF_FA:OFF_FA + A_WIDTH]
    sig = _sigmoid(z)
    f = lb + (1.0 - lb) * sig
    logf = jnp.log2(jnp.maximum(f, LOG_FLOOR)) * valid
    kk = (1.0 - lb) * (1.0 - sig) * valid
    proj_ref[:, OFF_QA:OFF_QA + A_WIDTH] = _silu(proj_ref[:, OFF_QA:OFF_QA + A_WIDTH])
    proj_ref[:, OFF_FA:OFF_FA + A_WIDTH] = kk
    lf_hi = logf.astype(BF16)
    lf_lo = (logf - lf_hi.astype(F32)).astype(BF16)
    expo_ref[0:TILE, :] = _dot(mstack_ref[...], jnp.concatenate([lf_hi, lf_lo], axis=0))
    expo_ref[TILE:2 * TILE, :] = jnp.where(valid > 0.0, jnp.maximum(f, LOG_FLOOR), 1.0)
    tick()

    def level_factors(bsz, g, f_row):
        half = bsz // 2
        if bsz == 2:
            return jnp.where((rloc & 1) == 1, f_row, 1.0)
        if bsz <= SUBLANES:
            sub = lax.broadcasted_iota(jnp.int32, (SUBLANES, 1), 0)
            vregs = []
            for r0 in range(0, TILE, SUBLANES):
                gv = g[r0:r0 + SUBLANES]
                piv = None
                for b0 in range(0, SUBLANES, bsz):
                    row = jnp.broadcast_to(gv[b0 + half - 1:b0 + half, :], gv.shape)
                    piv = row if piv is None else jnp.where(sub >= b0, row, piv)
                vregs.append(gv - piv)
            d = jnp.concatenate(vregs, axis=0)
            return jnp.exp2(jnp.where((rloc & (bsz - 1)) >= half, d, -d))
        parts = []
        for b0 in range(0, TILE, bsz):
            piv = jnp.broadcast_to(g[b0 + half - 1:b0 + half, :], (half, g.shape[1]))
            parts += [piv - g[b0:b0 + half], g[b0 + half:b0 + bsz] - piv]
        return jnp.exp2(jnp.concatenate(parts, axis=0))

    def hgrn_heads(hds):
        cs = {hd: slice(hd * DK_A, (hd + 1) * DK_A) for hd in hds}
        qf_h = {hd: proj_ref[:, OFF_QA + hd * DK_A:OFF_QA + (hd + 1) * DK_A] for hd in hds}
        kk_h = {hd: proj_ref[:, OFF_FA + hd * DK_A:OFF_FA + (hd + 1) * DK_A] for hd in hds}
        diag = {hd: jnp.sum(qf_h[hd] * kk_h[hd], axis=-1, keepdims=True) for hd in hds}
        attn = {}
        g_inc = {hd: expo_ref[0:TILE, cs[hd]] for hd in hds}
        f_row = {hd: expo_ref[TILE:2 * TILE, cs[hd]] for hd in hds}
        for li, bsz in enumerate(LEVELS):
            upper = (rloc & (bsz - 1)) >= (bsz // 2)
            for hd in hds:
                xb = (jnp.where(upper, qf_h[hd], kk_h[hd])
                      * level_factors(bsz, g_inc[hd], f_row[hd])).astype(BF16)
                term = _dot_nt(xb, xb) * lmask_ref[li]
                attn[hd] = term if li == 0 else attn[hd] + term
            if li == len(LEVELS) // 2:
                tick()
        v_f32 = {hd: proj_ref[:, OFF_IA + hd * DK_A:OFF_IA + (hd + 1) * DK_A] for hd in hds}
        v_h = {hd: v_f32[hd].astype(BF16) for hd in hds}
        st = {hd: st_ref[hd] for hd in hds}
        o = {hd: _dot(attn[hd].astype(BF16), v_h[hd]) + diag[hd] * v_f32[hd] for hd in hds}
        for hd in hds:
            q_dec = (qf_h[hd] * jnp.exp2(g_inc[hd])).astype(BF16)
            o[hd] = o[hd] + _dot_nt(q_dec, st[hd].astype(BF16))
        for hd in hds:
            g_tot = g_inc[hd][TILE - 1:TILE]
            k_dec = (kk_h[hd] * jnp.exp2(g_tot - g_inc[hd])).astype(BF16)
            st_ref[hd] = jnp.exp2(g_tot) * st[hd] + _dot_tn(v_h[hd], k_dec)
        for _ in range(len(hds) - 1):
            tick()
        for hd in hds:
            ya = o[hd] * lax.rsqrt(jnp.mean(o[hd] * o[hd], axis=-1, keepdims=True) + RMS_EPS) * hg_ref[lyr, :]
            ga = proj_ref[:, OFF_GA + hd * DK_A:OFF_GA + (hd + 1) * DK_A]
            mixed_ref[:, hd * DK_A:(hd + 1) * DK_A] = (ya * _silu(ga)).astype(BF16)

    for hd0 in range(0, HA, HGRN_GROUP):
        hgrn_heads(tuple(range(hd0, hd0 + HGRN_GROUP)))

    lane = lax.broadcasted_iota(jnp.int32, (1, LANES), 1)
    lo_half = lane < DH_B
    first = (lane & (DH_B - 1)) < DH_B // 2
    cos = cos_ref[trig_rows, :]
    sin = sin_ref[trig_rows, :]

    def norm_rope(x, g):
        sq = x * x
        s0 = jnp.sum(jnp.where(lo_half, sq, 0.0), axis=-1, keepdims=True)
        s1 = jnp.sum(jnp.where(lo_half, 0.0, sq), axis=-1, keepdims=True)
        msq = jnp.where(lo_half, s0, s1) * (1.0 / DH_B)
        y = x * lax.rsqrt(msq + RMS_EPS) * g
        rot = jnp.where(first, pltpu.roll(y, LANES - DH_B // 2, axis=1),
                        pltpu.roll(y, DH_B // 2, axis=1))
        return y * cos + rot * sin

    kc = norm_rope(proj_ref[:, OFF_KB:OFF_KB + KV_WIDTH], kg_ref[lyr, :])
    vc = proj_ref[:, OFF_VB:OFF_VB + KV_WIDTH]
    is_meta_tile = blk == 0
    for buf, val in ((kbuf_ref, kc), (ksw_ref, pltpu.roll(kc, DH_B, axis=1)),
                     (vbuf_ref, vc), (vsw_ref, pltpu.roll(vc, DH_B, axis=1))):
        val = val.astype(BF16)
        buf[KB_CUR:KB_CUR + TILE, :] = val
        for m0 in (KB_META0, KB_META1):
            buf[m0:m0 + N_META, :] = jnp.where(is_meta_tile, val[PAD_FRONT:], buf[m0:m0 + N_META, :])
    tick()

    scale = DH_B ** -0.5 * LOG2E
    qcols = []
    for j in range(HB // 2):
        qj = norm_rope(proj_ref[:, OFF_QB + j * LANES:OFF_QB + (j + 1) * LANES], qg_ref[lyr, :]) * scale
        qcols.append((jnp.where(lo_half, qj, 0.0).astype(BF16), jnp.where(lo_half, 0.0, qj).astype(BF16)))
    tick()

    grp = HB // KVH_B
    heads_same = [hh for hh in range(HB) if (hh % 2) == (hh // grp)]
    heads_swap = [hh for hh in range(HB) if (hh % 2) != (hh // grp)]
    col = lax.broadcasted_iota(jnp.int32, (1, KWIN), 1)
    big = jnp.int32(1 << 30)
    windows = ((0, 0), (KBUF_ROWS - KWIN, KWIN - 1))

    def attn_sub_block(sb):
        w0, sink_col = windows[sb]
        q0 = sb * QB
        brow = w0 + col
        in_meta0 = (brow >= KB_META0) & (brow < KB_PREV)
        in_meta1 = (brow >= KB_META1) & (brow < KB_META1 + N_META)
        in_band = (brow >= KB_PREV) & (brow < KB_META1)
        kj = jnp.where(in_meta0, PAD_FRONT + brow - KB_META0,
                       jnp.where(in_meta1, PAD_FRONT + brow - KB_META1,
                                 jnp.where(in_band, (blk - 1) * TILE + brow - KB_PREV, big)))
        qi = blk * TILE + q0 + lax.broadcasted_iota(jnp.int32, (QB, 1), 0)
        allowed = (kj <= qi) & (in_meta0 | in_meta1 | ((kj > qi - TILE) & (kj >= TILE)))

        groups = ((heads_same, kbuf_ref, vbuf_ref), (heads_swap, ksw_ref, vsw_ref))
        for g, (heads, k_ref, _) in enumerate(groups):
            lhs = jnp.concatenate([qcols[hh // 2][hh % 2][q0:q0 + QB] for hh in heads], axis=0)
            score_ref[g] = _dot_nt(lhs, k_ref[w0:w0 + KWIN, :])
        s_all = [score_ref.at[g] for g in range(len(groups))]
        ps, inv_den = [[] for _ in groups], [[] for _ in groups]
        for n in range(len(heads_same)):
            for g, (heads, _, _) in enumerate(groups):
                fill = jnp.where(col == sink_col, sinks_ref[layer, heads[n]] * LOG2E, NEG_INF)
                s = jnp.where(allowed, s_all[g][n * QB:(n + 1) * QB], fill)
                pr = jnp.exp2(s - jnp.max(s, axis=-1, keepdims=True))
                inv_den[g].append(1.0 / jnp.sum(pr, axis=-1, keepdims=True))
                ps[g].append(pr.astype(BF16))
            if n == len(heads_same) // 2 - 1:
                tick()
        outs = {}
        for g, (heads, _, v_ref) in enumerate(groups):
            pv_ref[g] = _dot(jnp.concatenate(ps[g], axis=0), v_ref[w0:w0 + KWIN, :])
            for n, hh in enumerate(heads):
                outs[hh] = pv_ref[g, n * QB:(n + 1) * QB, :] * inv_den[g][n]
        tick()
        for j in range(HB // 2):
            yb = jnp.where(lo_half, outs[2 * j], outs[2 * j + 1])
            gb = proj_ref[q0:q0 + QB, OFF_GB + j * LANES:OFF_GB + (j + 1) * LANES]
            mixed_ref[q0:q0 + QB, A_WIDTH + j * LANES:A_WIDTH + (j + 1) * LANES] = (
                yb * _silu(gb)).astype(BF16)

    for sb in range(len(windows)):
        attn_sub_block(sb)

    for buf in (kbuf_ref, ksw_ref, vbuf_ref, vsw_ref):
        buf[KB_PREV:KB_CUR, :] = buf[KB_CUR:KB_CUR + TILE, :]

    wmax = max(POOL_WINDOWS)
    ug = proj_ref[:, OFF_UC:OFF_UC + C_WIDTH] * valid
    uext_ref[wmax:wmax + TILE, :] = ug
    n_valid = jnp.maximum(row - (PAD_FRONT - 1), 0)
    for g, w in enumerate(POOL_WINDOWS):
        cs = slice(g * CG_C, (g + 1) * CG_C)
        acc = uext_ref[wmax:wmax + TILE, cs]
        for s in range(1, w):
            acc = acc + uext_ref[wmax - s:wmax - s + TILE, cs]
        cnt = (n_valid - jnp.maximum(row - w - (PAD_FRONT - 1), 0)).astype(F32)
        pooled = (acc / jnp.maximum(cnt, 1.0) - ug[:, cs]) * valid
        y = _dot(pooled.astype(BF16), pw_ref[layer, g].astype(BF16)) * ps_ref[lyr, cs]
        gc = proj_ref[:, OFF_GC + g * CG_C:OFF_GC + (g + 1) * CG_C]
        mixed_ref[:, A_WIDTH + B_WIDTH + g * CG_C:A_WIDTH + B_WIDTH + (g + 1) * CG_C] = (
            y * _silu(gc)).astype(BF16)
        tick()
    uext_ref[0:wmax, :] = ug[TILE - wmax:]


def _layer_kernel(layer, first, sinks_ref, *refs):
    n_h = 4 if first else 2
    h_refs, refs = refs[:n_h], refs[n_h:]
    (cos_ref, sin_ref, ng_ref, win_hbm, lbl_ref, qg_ref, kg_ref, hg_ref, pw_ref,
     ps_ref, wout_hbm, mstack_ref, lmask_ref, out_ref, win_ref, wout_ref, wsem, xn_ref, proja_ref,
     projb_ref, mixeda_ref, mixedb_ref, expo_ref, st_ref, kbuf_ref, ksw_ref, vbuf_ref, vsw_ref,
     uext_ref, score_ref, pv_ref) = refs
    t = pl.program_id(0)

    @pl.when(t == 0)
    def _load_weights_and_zero_state():
        stage = (proja_ref, projb_ref)
        for w_hbm, w_vmem in ((win_hbm, win_ref), (wout_hbm, wout_ref)):
            width = w_vmem.shape[1]
            n_chunks = w_vmem.shape[0] // TILE

            def chunk_copy(k, slot, w_hbm=w_hbm, width=width):
                return pltpu.make_async_copy(w_hbm.at[layer, pl.ds(k * TILE, TILE), :],
                                             stage[slot].at[:, 0:width], wsem.at[slot])

            chunk_copy(0, 0).start()
            chunk_copy(1, 1).start()

            def pair(i, carry, w_vmem=w_vmem, width=width, n_chunks=n_chunks, chunk_copy=chunk_copy):
                for slot in (0, 1):
                    k = 2 * i + slot
                    chunk_copy(k, slot).wait()
                    w_vmem[pl.ds(pl.multiple_of(k * TILE, TILE), TILE), :] = stage[slot][:, 0:width].astype(BF16)

                    @pl.when(k + 2 < n_chunks)
                    def _next():
                        chunk_copy(k + 2, slot).start()
                return carry

            lax.fori_loop(0, n_chunks // 2, pair, 0)

        for ref in (proja_ref, projb_ref, mixeda_ref, mixedb_ref, st_ref, kbuf_ref, ksw_ref,
                    vbuf_ref, vsw_ref, uext_ref):
            ref[...] = jnp.zeros_like(ref)

    def step_rows(step_idx, rws, cols, x_ref, meta_ref=None):
        x = x_ref[rws, cols]
        if not first:
            return x
        if rws.stop is not None and rws.stop < STEP:
            return jnp.where(step_idx == 0, 0.0, x)
        n_rows = x.shape[0]
        front = jnp.concatenate([jnp.zeros((n_rows - N_META, x.shape[1]), F32), meta_ref[:, cols]], axis=0)
        return jnp.where(step_idx == 0, front, x)

    d_model = out_ref.shape[1]
    last_step = pl.num_programs(0) - 1
    rows_a, rows_b = slice(0, TILE), slice(TILE, STEP)

    def norm_rows():
        h = step_rows(t, slice(None), slice(None), *h_refs[:n_h // 2])
        ms = jnp.mean(h * h, axis=-1, keepdims=True)
        xn_ref[...] = (h * lax.rsqrt(ms + RMS_EPS) * ng_ref[layer:layer + 1, :]).astype(BF16)

    def out_chunk(rws, mixed_src, c0):
        cols = slice(c0, c0 + MM_CHUNK)
        h_res = step_rows(t - 1, rws, cols, *h_refs[n_h // 2:])
        out_ref[rws, cols] = h_res + _dot(mixed_src[...], wout_ref[:, cols])

    def in_chunk(rws, proj_dst, c0):
        cols = slice(c0, c0 + MM_CHUNK)
        proj_dst[:, cols] = _dot(xn_ref[rws, :], win_ref[:, cols])

    def out_chunks(rws, mixed_src):
        return [functools.partial(out_chunk, rws, mixed_src, c0) for c0 in range(0, d_model, MM_CHUNK)]

    def in_chunks(rws, proj_dst):
        return [functools.partial(in_chunk, rws, proj_dst, c0) for c0 in range(0, IN_COLS, MM_CHUNK)]

    def half_step(work, blk_mix, trig, proj_mix, mixed_mix):
        progress = {"ticks": 0, "done": 0}

        def tick():
            progress["ticks"] += 1
            target = min(len(work), -(-len(work) * progress["ticks"] // (MIX_TICKS + 1)))
            while progress["done"] < target:
                work[progress["done"]]()
                progress["done"] += 1

        tick()
        _mix_tile(layer, blk_mix, sinks_ref, cos_ref, sin_ref, trig, lbl_ref, qg_ref, kg_ref, hg_ref,
                  pw_ref, ps_ref, mstack_ref, lmask_ref, proj_mix, mixed_mix, expo_ref, st_ref,
                  kbuf_ref, ksw_ref, vbuf_ref, vsw_ref, uext_ref, score_ref, pv_ref, tick)
        assert progress["done"] == len(work), "MIX_TICKS must match the tick() calls in _mix_tile"

    trig_b, trig_a = rows_a, rows_b

    @pl.when(t == 0)
    def _first_step():
        norm_rows()
        for chunk in in_chunks(rows_b, projb_ref):
            chunk()

    @pl.when((t > 0) & (t < last_step))
    def _main_step():
        norm_rows()
        half_step(out_chunks(rows_a, mixeda_ref) + in_chunks(rows_a, proja_ref), 2 * t - 2, trig_b,
                  projb_ref, mixedb_ref)
        half_step(out_chunks(rows_b, mixedb_ref) + in_chunks(rows_b, projb_ref), 2 * t - 1, trig_a,
                  proja_ref, mixeda_ref)

    @pl.when(t == last_step)
    def _drain_step():
        half_step(out_chunks(rows_a, mixeda_ref), 2 * t - 2, trig_b, projb_ref, mixedb_ref)
        for chunk in out_chunks(rows_b, mixedb_ref):
            chunk()


def _layer_call(layer, first, last, h_in, meta, cos, sin, sinks, norm_g, w_in, lb_logits, qg, kg, hg,
                pool_w, pool_scale, w_out, mstack, lmask):
    d = h_in.shape[1]
    n_steps = cos.shape[0] // STEP - 1

    def rows(width, lag, skip_front=False):
        off = lag + (1 if skip_front else 0)
        hi = n_steps - 1 - (1 if skip_front else 0)
        return pl.BlockSpec((STEP, width), lambda t: (jnp.clip(t - off, 0, hi), 0))

    def whole(shape, single=False):
        idx = lambda t: (0,) * len(shape)
        if single:
            return pl.BlockSpec(shape, idx, pipeline_mode=pl.Buffered(1))
        return pl.BlockSpec(shape, idx)

    in_specs = [pl.BlockSpec(memory_space=pltpu.SMEM)]
    args = [sinks]
    for lag in (0, 1):
        in_specs.append(rows(d, lag, skip_front=first))
        args.append(h_in)
        if first:
            in_specs.append(whole(meta.shape))
            args.append(meta)
    in_specs += [
        pl.BlockSpec((STEP, LANES), lambda t: (t, 0)),
        pl.BlockSpec((STEP, LANES), lambda t: (t, 0)),
        whole(norm_g.shape),
        pl.BlockSpec(memory_space=pl.ANY),
        whole(lb_logits.shape),
        whole(qg.shape), whole(kg.shape), whole(hg.shape),
        whole(pool_w.shape), whole(pool_scale.shape),
        pl.BlockSpec(memory_space=pl.ANY),
        whole(mstack.shape, single=True), whole(lmask.shape, single=True),
    ]
    args += [cos, sin, norm_g, w_in, lb_logits, qg, kg, hg, pool_w, pool_scale, w_out, mstack, lmask]
    out_rows = (n_steps - 1) * STEP if last else n_steps * STEP
    assert w_in.shape[1] % (2 * TILE) == 0 and w_out.shape[1] % (2 * TILE) == 0
    assert w_out.shape[2] <= IN_COLS
    scratch = [
        pltpu.VMEM(w_in.shape[1:], BF16),
        pltpu.VMEM(w_out.shape[1:], BF16),
        pltpu.SemaphoreType.DMA((2,)),
        pltpu.VMEM((STEP, d), BF16),
        pltpu.VMEM((TILE, IN_COLS), F32),
        pltpu.VMEM((TILE, IN_COLS), F32),
        pltpu.VMEM((TILE, MIX_WIDTH), BF16),
        pltpu.VMEM((TILE, MIX_WIDTH), BF16),
        pltpu.VMEM((EXPO_BLOCKS * TILE, A_WIDTH), F32),
        pltpu.VMEM((HA, DK_A, DK_A), F32),
        pltpu.VMEM((KBUF_ROWS, LANES), BF16),
        pltpu.VMEM((KBUF_ROWS, LANES), BF16),
        pltpu.VMEM((KBUF_ROWS, LANES), BF16),
        pltpu.VMEM((KBUF_ROWS, LANES), BF16),
        pltpu.VMEM((max(POOL_WINDOWS) + TILE, C_WIDTH), F32),
        pltpu.VMEM((KVH_B, (HB // KVH_B) * QB, KWIN), F32),
        pltpu.VMEM((KVH_B, (HB // KVH_B) * QB, LANES), F32),
    ]
    return pl.pallas_call(
        functools.partial(_layer_kernel, layer, first),
        grid=(n_steps + 1,),
        in_specs=in_specs,
        out_specs=rows(d, 1, skip_front=last),
        out_shape=jax.ShapeDtypeStruct((out_rows, d), F32),
        scratch_shapes=scratch,
        compiler_params=pltpu.CompilerParams(
            dimension_semantics=("arbitrary",), vmem_limit_bytes=VMEM_LIMIT),
        name=f"hybrid_layer{layer}",
    )(*args)


def kernel(x, meta_tokens, lb_logits, norm_g, w_in, q_norm_g, k_norm_g, attn_sinks, hgrn_norm_g,
           pool_w, pool_scale, w_out):
    b, seq, d = x.shape
    depth = w_in.shape[0]
    assert b == 1 and seq % STEP == 0
    assert w_in.shape[2] == IN_COLS and w_out.shape[1] == MIX_WIDTH
    assert meta_tokens.shape[0] == N_META

    p = 2 * STEP + seq
    pos = np.arange(p, dtype=np.float64) - TILE - (STEP - N_META)
    half = DH_B // 2
    inv = np.power(ROPE_THETA, -np.arange(half, dtype=np.float64) * 2.0 / DH_B)
    ang = pos[:, None] * inv[None, :]
    cos = jnp.asarray(np.tile(np.cos(ang), (1, LANES // half)), F32)
    sin = jnp.asarray(np.tile(np.concatenate([-np.sin(ang), np.sin(ang)], axis=1), (1, LANES // DH_B)), F32)

    mstack_np, lmask_np = _decay_constants()
    mstack = jnp.asarray(mstack_np, BF16)
    lmask = jnp.asarray(lmask_np, F32)

    qg = jnp.tile(q_norm_g, (1, LANES // DH_B))
    kg = jnp.tile(k_norm_g, (1, LANES // DH_B))
    h = x[0]
    for l in range(depth):
        h = _layer_call(l, l == 0, l == depth - 1, h, meta_tokens.astype(F32), cos, sin, attn_sinks,
                        norm_g, w_in, lb_logits, qg, kg, hgrn_norm_g, pool_w, pool_scale, w_out,
                        mstack, lmask)
    return h[None]
```
